```python
import math
import jax, jax.numpy as jnp
from jax import lax
import numpy as np

D_MODEL = 1024
BATCH = 1
SEQ = 16384
DEPTH = 1
DEC_BATCH = 8
DEC_SEQ = 4096
PAST_LEN = 128

GLA_HEADS = 4
GLA_DK = D_MODEL // 2
GLA_DV = D_MODEL
GLA_HK = GLA_DK // GLA_HEADS
GLA_HV = GLA_DV // GLA_HEADS
GLA_RANK = 16
GLA_LOGIT_NORM = 16.0
GLA_CHUNK = 64
DIFF_HEAD_DIM = 64
DIFF_HEADS = D_MODEL // (2 * DIFF_HEAD_DIM)
DIFF_QK = DIFF_HEADS * 2 * DIFF_HEAD_DIM
DIFF_DV = DIFF_HEADS * 2 * DIFF_HEAD_DIM
ROPE_THETA = 10000.0
Q_BLOCK = 128
EPS = 1e-6

SPLIT_SIZES = (GLA_DK, GLA_DK, GLA_DV, GLA_DV, 2 * GLA_RANK,
               DIFF_QK, DIFF_QK, DIFF_DV, DIFF_DV, D_MODEL, D_MODEL)
IN_COLS = sum(SPLIT_SIZES)

kernel_name = "hybrid_gla_diffattn_encoder"


def rms_norm(x, gain):
    xf = x.astype(jnp.float32)
    y = xf * lax.rsqrt(jnp.mean(xf * xf, axis=-1, keepdims=True) + EPS)
    return y.astype(x.dtype) * gain


def rope(x):
    L, d = x.shape[1], x.shape[-1]
    half = d // 2
    inv = 1.0 / (ROPE_THETA ** (jnp.arange(half, dtype=jnp.float32) / half))
    ang = jnp.arange(L, dtype=jnp.float32)[:, None] * inv[None, :]
    cos = jnp.cos(ang)[None, :, None, None, :]
    sin = jnp.sin(ang)[None, :, None, None, :]
    xf = x.astype(jnp.float32)
    x1, x2 = xf[..., :half], xf[..., half:]
    out = jnp.concatenate([x1 * cos - x2 * sin, x1 * sin + x2 * cos], axis=-1)
    return out.astype(x.dtype)


def gla_chunk_step(state, inp):
    q, k, v, g = inp
    C = q.shape[-2]
    b = jnp.cumsum(g, axis=-2)
    o_inter = jnp.einsum('zbhcd,zbhde->zbhce', q * jnp.exp(b), state)
    causal = jnp.tril(jnp.ones((C, C), dtype=bool))
    rel = b[..., :, None, :] - b[..., None, :, :]
    decay = jnp.exp(jnp.where(causal[:, :, None], rel, -jnp.inf))
    scores = jnp.einsum('zbhid,zbhjd,zbhijd->zbhij', q, k, decay)
    o = o_inter + jnp.einsum('zbhij,zbhje->zbhie', scores, v)
    b_last = b[..., -1:, :]
    state = (jnp.exp(b_last[..., 0, :])[..., None] * state
             + jnp.einsum('zbhcd,zbhce->zbhde', k * jnp.exp(b_last - b), v))
    return state, o


def bidirectional_gla(q, k, v, g_fwd, g_bwd):
    B, L, H, dk = q.shape
    dv = v.shape[-1]
    n = L // GLA_CHUNK
    flip = lambda t: jnp.flip(t, axis=1)

    def to_chunks(fwd, bwd):
        t = jnp.stack([fwd, bwd], axis=0).astype(jnp.float32)
        t = t.reshape(2, B, n, GLA_CHUNK, H, t.shape[-1])
        return t.transpose(2, 0, 1, 4, 3, 5)

    xs = (to_chunks(q, flip(q)), to_chunks(k, flip(k)),
          to_chunks(v, flip(v)), to_chunks(g_fwd, flip(g_bwd)))
    state0 = jnp.zeros((2, B, H, dk, dv), jnp.float32)
    _, o = lax.scan(gla_chunk_step, state0, xs)
    o = o.transpose(1, 2, 0, 4, 3, 5).reshape(2, B, L, H, dv)
    return (o[0] + flip(o[1])).astype(v.dtype)


def diff_attention(q, k, v, lam):
    B, L, H, _, d = q.shape
    nb = L // Q_BLOCK
    scale = d ** -0.5
    qb = q.reshape(B, nb, Q_BLOCK, H, 2, d).transpose(1, 0, 2, 3, 4, 5)

    def block(qblk):
        s = jnp.einsum('bqhzd,bkhzd->bhzqk', qblk, k,
                       preferred_element_type=jnp.float32) * scale
        p = jax.nn.softmax(s, axis=-1)
        p = p[:, :, 0] - lam * p[:, :, 1]
        return jnp.einsum('bhqk,bkhe->bqhe', p.astype(v.dtype), v)

    o = lax.map(block, qb)
    return o.transpose(1, 0, 2, 3, 4).reshape(B, L, H, 2 * d)


def encoder_layer(x, c, lam_init, w_ada, b_ada, norm_gain, w_in, w_alpha, b_alpha,
                  gla_norm_gain, lambda_q, lambda_k, diff_norm_gain,
                  w_bo_gla, w_bo_diff, w_out):
    B, L, _ = x.shape
    mod = jax.nn.silu(c) @ w_ada + b_ada
    shift, scale, gate = jnp.split(mod, 3, axis=-1)
    h = rms_norm(x, norm_gain) * (1.0 + scale[:, None, :]) + shift[:, None, :]

    proj = h @ w_in
    points = np.cumsum(SPLIT_SIZES)[:-1].tolist()
    (a_q, a_k, a_v, a_z, a_low, d_q, d_k, d_v, d_z, m_gla, m_diff) = jnp.split(proj, points, axis=-1)

    q = (a_q * (GLA_HK ** -0.5)).reshape(B, L, GLA_HEADS, GLA_HK)
    k = a_k.reshape(B, L, GLA_HEADS, GLA_HK)
    v = a_v.reshape(B, L, GLA_HEADS, GLA_HV)
    low = a_low.reshape(B, L, 2, GLA_RANK)
    logits = jnp.einsum('blzr,zrk->blzk', low, w_alpha) + b_alpha
    log_alpha = jax.nn.log_sigmoid(logits.astype(jnp.float32)) / GLA_LOGIT_NORM
    g_fwd = log_alpha[:, :, 0].reshape(B, L, GLA_HEADS, GLA_HK)
    g_bwd = log_alpha[:, :, 1].reshape(B, L, GLA_HEADS, GLA_HK)
    o_gla = bidirectional_gla(q, k, v, g_fwd, g_bwd)
    o_gla = rms_norm(o_gla, gla_norm_gain).reshape(B, L, GLA_DV) * jax.nn.silu(a_z)
    y_gla = o_gla @ w_bo_gla

    lq = lambda_q.astype(jnp.float32)
    lk = lambda_k.astype(jnp.float32)
    lam = jnp.exp(jnp.sum(lq[0] * lk[0])) - jnp.exp(jnp.sum(lq[1] * lk[1])) + lam_init
    dq = rope(d_q.reshape(B, L, DIFF_HEADS, 2, DIFF_HEAD_DIM))
    dk = rope(d_k.reshape(B, L, DIFF_HEADS, 2, DIFF_HEAD_DIM))
    dv = d_v.reshape(B, L, DIFF_HEADS, 2 * DIFF_HEAD_DIM)
    o_diff = diff_attention(dq, dk, dv, lam)
    o_diff = rms_norm(o_diff, diff_norm_gain) * (1.0 - lam_init)
    o_diff = o_diff.reshape(B, L, DIFF_DV) * jax.nn.silu(d_z)
    y_diff = o_diff @ w_bo_diff

    merged = jax.nn.sigmoid(m_gla) * y_gla + jax.nn.sigmoid(m_diff) * y_diff
    return x + gate[:, None, :] * (merged @ w_out)


def setup_inputs(seed: int = 0) -> dict:
    key = jax.random.key(seed)
    ks = jax.random.split(key, 20)
    f32 = jnp.float32
    nrm = lambda k, shape, s: jax.random.normal(k, shape, f32) * s
    return {
        "x_prompt": nrm(ks[0], (BATCH, SEQ, D_MODEL), 1.0),
        "x_sample": nrm(ks[1], (DEC_BATCH, DEC_SEQ, D_MODEL), 1.0),
        "c_prompt": nrm(ks[2], (BATCH, D_MODEL), 1.0),
        "c_sample": nrm(ks[3], (DEC_BATCH, D_MODEL), 1.0),
        "w_ada": nrm(ks[4], (DEPTH, D_MODEL, 3 * D_MODEL), D_MODEL ** -0.5),
        "b_ada": nrm(ks[5], (DEPTH, 3 * D_MODEL), 0.01),
        "norm_gain": 1.0 + nrm(ks[6], (DEPTH, D_MODEL), 0.01),
        "w_in": nrm(ks[7], (DEPTH, D_MODEL, IN_COLS), D_MODEL ** -0.5),
        "w_alpha": nrm(ks[8], (DEPTH, 2, GLA_RANK, GLA_DK), GLA_RANK ** -0.5),
        "b_alpha": nrm(ks[9], (DEPTH, 2, GLA_DK), 0.1),
        "gla_norm_gain": 1.0 + nrm(ks[10], (DEPTH, GLA_HV), 0.01),
        "lambda_q": nrm(ks[11], (DEPTH, 2, DIFF_HEAD_DIM), 0.1),
        "lambda_k": nrm(ks[12], (DEPTH, 2, DIFF_HEAD_DIM), 0.1),
        "diff_norm_gain": 1.0 + nrm(ks[13], (DEPTH, 2 * DIFF_HEAD_DIM), 0.01),
        "w_bo_gla": nrm(ks[14], (DEPTH, GLA_DV, D_MODEL), GLA_DV ** -0.5),
        "w_bo_diff": nrm(ks[15], (DEPTH, DIFF_DV, D_MODEL), DIFF_DV ** -0.5),
        "w_out": nrm(ks[16], (DEPTH, D_MODEL, D_MODEL), D_MODEL ** -0.5),
        "final_gain": 1.0 + nrm(ks[17], (D_MODEL,), 0.01),
    }


def reference(x_prompt, x_sample, c_prompt, c_sample, w_ada, b_ada, norm_gain, w_in,
              w_alpha, b_alpha, gla_norm_gain, lambda_q, lambda_k, diff_norm_gain,
              w_bo_gla, w_bo_diff, w_out, final_gain):
    def trunk(x, c):
        for layer in range(DEPTH):
            lam_init = 0.8 - 0.6 * math.exp(-0.3 * layer)
            x = encoder_layer(x, c, lam_init, w_ada[layer], b_ada[layer], norm_gain[layer],
                              w_in[layer], w_alpha[layer], b_alpha[layer],
                              gla_norm_gain[layer], lambda_q[layer], lambda_k[layer],
                              diff_norm_gain[layer], w_bo_gla[layer], w_bo_diff[layer],
                              w_out[layer])
        return rms_norm(x, final_gain)

    y_prompt = trunk(x_prompt, c_prompt)
    y_sample = trunk(x_sample, c_sample)
    return (y_prompt, y_sample)
```

```python
import functools
import math

import numpy as np
import jax
import jax.numpy as jnp
from jax import lax
from jax.experimental import pallas as pl
from jax.experimental.pallas import tpu as pltpu

D_MODEL = 1024
GLA_HEADS = 4
GLA_DK = D_MODEL // 2
GLA_DV = D_MODEL
GLA_HK = GLA_DK // GLA_HEADS
GLA_HV = GLA_DV // GLA_HEADS
GLA_RANK = 16
GLA_LOGIT_NORM = 16.0
GLA_CHUNK = 64
GLA_SUB = 8
DIFF_HEAD_DIM = 64
DIFF_HEADS = D_MODEL // (2 * DIFF_HEAD_DIM)
DIFF_HV = 2 * DIFF_HEAD_DIM
ROPE_THETA = 10000.0
EPS = 1e-6
LANES = 128

F32 = jnp.float32
BF16 = jnp.bfloat16

VMEM_LIMIT_BYTES = 56 * 1024 * 1024

NT_DIMS = (((1,), (1,)), ((), ()))
TN_DIMS = (((0,), (0,)), ((), ()))


def _sigmoid(x):
    return 1.0 / (1.0 + jnp.exp(-x))


def _const_spec(shape):
    nd = len(shape)
    return pl.BlockSpec(shape, lambda *_: (0,) * nd, pipeline_mode=pl.Buffered(1))


def _ada_kernel(c_ref, w_ref, b_ref, o_ref):
    c = c_ref[...]
    s = c * _sigmoid(c)
    o_ref[...] = jnp.dot(s.astype(BF16), w_ref[...], preferred_element_type=F32) + b_ref[...]


def _ada(c, w_bf16, b):
    rows = c.shape[0]
    n_out = w_bf16.shape[1]
    tn = D_MODEL
    return pl.pallas_call(
        _ada_kernel,
        grid=(n_out // tn,),
        in_specs=[
            pl.BlockSpec((rows, D_MODEL), lambda j: (0, 0)),
            pl.BlockSpec((D_MODEL, tn), lambda j: (0, j)),
            pl.BlockSpec((1, tn), lambda j: (0, j)),
        ],
        out_specs=pl.BlockSpec((rows, tn), lambda j: (0, j)),
        out_shape=jax.ShapeDtypeStruct((rows, n_out), F32),
        name="ada",
    )(c, w_bf16, b)


def _rope_slices(x, cs, sn):
    lane = lax.broadcasted_iota(jnp.int32, (x.shape[0], LANES), 1)
    is_x2 = (lane & 32) != 0
    outs = []
    for s in range(x.shape[1] // LANES):
        xs = x[:, s * LANES:(s + 1) * LANES]
        from_lo = pltpu.roll(xs, 32, axis=1)
        from_hi = pltpu.roll(xs, LANES - 32, axis=1)
        partner = jnp.where(is_x2, from_lo, from_hi)
        outs.append(xs * cs + partner * sn)
    return jnp.concatenate(outs, axis=1)


def _inproj_kernel(x_ref, shift_ref, scale_ref, gain_ref, cs_ref, sn_ref, trif_ref, trib_ref,
                   wg_ref, wlow_ref, walpha_ref, balpha_ref,
                   wdq_ref, wdk_ref, wdv_ref, wdz_ref, wmg_ref, wmd_ref,
                   q_out, k_out, v_out, z_out, b_out,
                   dq_out, dk_out, dv_out, dz_out, mg_out, md_out):
    x = x_ref[0]
    ms = jnp.mean(x * x, axis=-1, keepdims=True)
    y = x * lax.rsqrt(ms + EPS) * gain_ref[...]
    h = (y * (1.0 + scale_ref[0]) + shift_ref[0]).astype(BF16)

    pg = jnp.dot(h, wg_ref[...], preferred_element_type=F32)
    q_out[0] = (pg[:, 0:GLA_DK] * (GLA_HK ** -0.5)).astype(BF16)
    k_out[0] = pg[:, GLA_DK:2 * GLA_DK].astype(BF16)
    v_out[0] = pg[:, 2 * GLA_DK:2 * GLA_DK + GLA_DV].astype(BF16)
    az = pg[:, 2 * GLA_DK + GLA_DV:]
    z_out[0] = (az * _sigmoid(az)).astype(BF16)

    low = jnp.dot(h, wlow_ref[...], preferred_element_type=F32)
    logits = jnp.dot(low.astype(BF16), walpha_ref[...], preferred_element_type=F32) + balpha_ref[...]
    g = (jnp.minimum(logits, 0.0) - jnp.log1p(jnp.exp(-jnp.abs(logits)))) * (1.0 / GLA_LOGIT_NORM)
    g_hi = g.astype(BF16)
    g_lo = (g - g_hi.astype(F32)).astype(BF16)
    for tri_ref, lo in ((trif_ref, 0), (trib_ref, GLA_DK)):
        tri = tri_ref[...]
        b_out[0, :, lo:lo + GLA_DK] = (
            jnp.dot(tri, g_hi[:, lo:lo + GLA_DK], preferred_element_type=F32)
            + jnp.dot(tri, g_lo[:, lo:lo + GLA_DK], preferred_element_type=F32))

    cs = cs_ref[...]
    sn = sn_ref[...]
    dq = jnp.dot(h, wdq_ref[...], preferred_element_type=F32)
    dq_out[0] = (_rope_slices(dq, cs, sn) * (DIFF_HEAD_DIM ** -0.5)).astype(BF16)
    dk = jnp.dot(h, wdk_ref[...], preferred_element_type=F32)
    dk_out[0] = _rope_slices(dk, cs, sn).astype(BF16)
    dv_out[0] = jnp.dot(h, wdv_ref[...], preferred_element_type=F32).astype(BF16)
    dz = jnp.dot(h, wdz_ref[...], preferred_element_type=F32)
    dz_out[0] = (dz * _sigmoid(dz)).astype(BF16)
    mg_out[0] = _sigmoid(jnp.dot(h, wmg_ref[...], preferred_element_type=F32)).astype(BF16)
    md_out[0] = _sigmoid(jnp.dot(h, wmd_ref[...], preferred_element_type=F32)).astype(BF16)


def _chunk_tri(tile, reverse):
    i = np.arange(tile)[:, None]
    j = np.arange(tile)[None, :]
    same = (i // GLA_CHUNK) == (j // GLA_CHUNK)
    tri = (j >= i) if reverse else (j <= i)
    return jnp.asarray(same & tri, dtype=BF16)


def _inproj(x, shift, scale, gain, cs, sn, weights, tile):
    bsz, seq, _ = x.shape
    (wg, wlow, walpha, balpha, wdq, wdk, wdv, wdz, wmg, wmd) = weights
    trif = _chunk_tri(tile, False)
    trib = _chunk_tri(tile, True)
    row = lambda c: pl.BlockSpec((1, tile, c), lambda b, i: (b, i, 0))
    per_b = pl.BlockSpec((1, 1, D_MODEL), lambda b, i: (b, 0, 0))
    consts = [gain, None, None, trif, trib, wg, wlow, walpha, balpha, wdq, wdk, wdv, wdz, wmg, wmd]
    in_specs = [row(D_MODEL), per_b, per_b]
    for a in consts:
        if a is None:
            in_specs.append(pl.BlockSpec((tile, LANES), lambda b, i: (i, 0)))
        else:
            in_specs.append(_const_spec(a.shape))
    out_cols = [(GLA_DK, BF16), (GLA_DK, BF16), (GLA_DV, BF16), (GLA_DV, BF16), (2 * GLA_DK, F32)] + \
               [(D_MODEL, BF16)] * 6
    return pl.pallas_call(
        _inproj_kernel,
        grid=(bsz, seq // tile),
        in_specs=in_specs,
        out_specs=[row(c) for c, _ in out_cols],
        out_shape=[jax.ShapeDtypeStruct((bsz, seq, c), dt) for c, dt in out_cols],
        compiler_params=pltpu.CompilerParams(
            dimension_semantics=("parallel", "parallel"), vmem_limit_bytes=VMEM_LIMIT_BYTES),
        name="inproj",
    )(x, shift, scale, gain, cs, sn, trif, trib, wg, wlow, walpha, balpha, wdq, wdk, wdv, wdz, wmg, wmd)


def _gla_masks(reverse):
    c = GLA_CHUNK
    i = np.arange(c)[:, None]
    j = np.arange(c)[None, :]
    q_half, k_half = (0, 1) if reverse else (1, 0)
    levels = []
    for s in (8, 16, 32):
        same_parent = (i // (2 * s)) == (j // (2 * s))
        levels.append(same_parent & ((i // s) % 2 == q_half) & ((j // s) % 2 == k_half))
    same_blk = (i // GLA_SUB) == (j // GLA_SUB)
    tri = (j >= i) if reverse else (j <= i)
    masks = levels + [same_blk & tri]
    return jnp.asarray(np.stack(masks).astype(np.float32))


def _gla_kernel(q_ref, k_ref, v_ref, b_ref, mask_ref, o_ref, state_ref, k32_ref, b32_ref, *,
                reverse, tile):
    c = GLA_CHUNK
    n_chunks = tile // c

    @pl.when(pl.program_id(1) == 0)
    def _():
        state_ref[...] = jnp.zeros_like(state_ref)

    lane64 = lax.broadcasted_iota(jnp.int32, (GLA_SUB, c), 1)
    scan_last = 0 if reverse else c - 1

    def chunk_body(ci, carry):
        cc = (n_chunks - 1 - ci) if reverse else ci
        row0 = pl.multiple_of(cc * c, c)
        rows = pl.ds(row0, c)
        for h in range(GLA_HEADS):
            lk = slice(h * GLA_HK, (h + 1) * GLA_HK)
            lv = slice(h * GLA_HV, (h + 1) * GLA_HV)
            q = q_ref[0, rows, lk].astype(F32)
            k = k_ref[0, rows, lk].astype(F32)
            v = v_ref[0, rows, lv]
            b = b_ref[0, rows, lk]
            k32_ref[h] = k
            b32_ref[h] = b

            def b_row(r):
                return b32_ref[h, pl.ds(r, 1), :]

            b_tot = b_row(scan_last)

            state = state_ref[h]
            q_in = (q * jnp.exp(b)).astype(BF16)
            o = jnp.dot(q_in, state.astype(BF16), preferred_element_type=F32)

            scores = jnp.zeros((c, c), F32)
            for lvl, s in enumerate((8, 16, 32)):
                pieces = []
                for p in range(c // (2 * s)):
                    m = 2 * s * p + (s if reverse else s - 1)
                    pieces.append(jnp.broadcast_to(b_row(m), (2 * s, GLA_HK)))
                ref = pieces[0] if len(pieces) == 1 else jnp.concatenate(pieces, axis=0)
                qt = (q * jnp.exp(jnp.minimum(b - ref, 0.0))).astype(BF16)
                kt = (k * jnp.exp(jnp.minimum(ref - b, 0.0))).astype(BF16)
                scores = scores + mask_ref[lvl] * lax.dot_general(
                    qt, kt, NT_DIMS, preferred_element_type=F32)

            blocks = []
            for jb in range(c // GLA_SUB):
                r0 = jb * GLA_SUB
                qb = q[r0:r0 + GLA_SUB]
                bb = b[r0:r0 + GLA_SUB]
                acc = jnp.zeros((GLA_SUB, c), F32)
                for jl in range(GLA_SUB):
                    col = r0 + jl
                    k_row = k32_ref[h, pl.ds(col, 1), :]
                    w = qb * k_row * jnp.exp(jnp.minimum(bb - b_row(col), 0.0))
                    acc = jnp.where(lane64 == col, jnp.sum(w, axis=1, keepdims=True), acc)
                blocks.append(acc)
            exact = jnp.concatenate(blocks, axis=0)
            scores = scores + mask_ref[3] * exact

            o = o + jnp.dot(scores.astype(BF16), v, preferred_element_type=F32)
            o_ref[0, rows, lv] = o.astype(o_ref.dtype)

            k_hat = (k * jnp.exp(b_tot - b)).astype(BF16)
            delta = lax.dot_general(k_hat, v, TN_DIMS, preferred_element_type=F32)
            decay_col = jnp.transpose(jnp.broadcast_to(jnp.exp(b_tot), (GLA_HK, GLA_HK)))
            decay = jnp.concatenate([decay_col] * (GLA_HV // GLA_HK), axis=1)
            state_ref[h] = state * decay + delta
        return carry

    lax.fori_loop(0, n_chunks, chunk_body, 0)


def _gla(q, k, v, bcum, *, reverse, tile):
    bsz, seq, _ = q.shape
    nt = seq // tile
    masks = _gla_masks(reverse)
    blk = (lambda b, t: (b, nt - 1 - t, 0)) if reverse else (lambda b, t: (b, t, 0))
    bcol = 1 if reverse else 0
    bblk = (lambda b, t: (b, nt - 1 - t, bcol)) if reverse else (lambda b, t: (b, t, bcol))
    kernel = functools.partial(_gla_kernel, reverse=reverse, tile=tile)
    return pl.pallas_call(
        kernel,
        grid=(bsz, nt),
        in_specs=[
            pl.BlockSpec((1, tile, GLA_DK), blk),
            pl.BlockSpec((1, tile, GLA_DK), blk),
            pl.BlockSpec((1, tile, GLA_DV), blk),
            pl.BlockSpec((1, tile, GLA_DK), bblk),
            _const_spec(masks.shape),
        ],
        out_specs=pl.BlockSpec((1, tile, GLA_DV), blk),
        out_shape=jax.ShapeDtypeStruct((bsz, seq, GLA_DV), BF16),
        scratch_shapes=[
            pltpu.VMEM((GLA_HEADS, GLA_HK, GLA_HV), F32),
            pltpu.VMEM((GLA_HEADS, GLA_CHUNK, GLA_HK), F32),
            pltpu.VMEM((GLA_HEADS, GLA_CHUNK, GLA_HK), F32),
        ],
        compiler_params=pltpu.CompilerParams(
            dimension_semantics=("parallel", "arbitrary"), vmem_limit_bytes=VMEM_LIMIT_BYTES),
        name="gla_bwd" if reverse else "gla_fwd",
    )(q, k, v, bcum, masks)


def _attn_kernel(q_ref, k_ref, v_ref, lq_ref, lk_ref, gain_ref, o_ref,
                 q2_ref, m_ref, l_ref, acc_ref, *, tq, tk, seq, lam_init):
    q = q_ref[0]
    lane = lax.broadcasted_iota(jnp.int32, (tq, LANES), 1)
    zero = jnp.zeros_like(q)
    q2_ref[0:tq] = jnp.where(lane < DIFF_HEAD_DIM, q, zero)
    q2_ref[tq:2 * tq] = jnp.where(lane >= DIFF_HEAD_DIM, q, zero)
    m_ref[...] = jnp.full_like(m_ref, -jnp.inf)
    l_ref[...] = jnp.zeros_like(l_ref)
    acc_ref[...] = jnp.zeros_like(acc_ref)

    def body(kt, carry):
        ks = pl.multiple_of(kt * tk, tk)
        k = k_ref[0, pl.ds(ks, tk), :]
        v = v_ref[0, pl.ds(ks, tk), :]
        s = lax.dot_general(q2_ref[...], k, NT_DIMS, preferred_element_type=F32)
        tiles = [s[:, j * LANES:(j + 1) * LANES] for j in range(tk // LANES)]
        m_prev = m_ref[...]
        m_tile = functools.reduce(jnp.maximum, tiles)
        m_new = jnp.maximum(m_prev, jnp.max(m_tile, axis=1, keepdims=True))
        alpha = jnp.exp(m_prev - m_new)
        p_tiles = [jnp.exp(t - m_new) for t in tiles]
        l_ref[...] = alpha * l_ref[...] + functools.reduce(jnp.add, p_tiles)
        p = jnp.concatenate([t.astype(BF16) for t in p_tiles], axis=1)
        acc_ref[...] = alpha * acc_ref[...] + jnp.dot(p, v, preferred_element_type=F32)
        m_ref[...] = m_new
        return carry

    lax.fori_loop(0, seq // tk, body, 0)

    l = jnp.sum(l_ref[...], axis=1, keepdims=True)
    o2 = acc_ref[...] / l
    dots = jnp.sum(lq_ref[...] * lk_ref[...], axis=1, keepdims=True)
    e = jnp.exp(dots)
    lam = e[0:1, :] - e[1:2, :] + lam_init
    o = o2[0:tq] - lam * o2[tq:2 * tq]
    ms = jnp.mean(o * o, axis=1, keepdims=True)
    o_ref[0] = (o * lax.rsqrt(ms + EPS) * gain_ref[...] * (1.0 - lam_init)).astype(o_ref.dtype)


def _attn(dq, dk, dv, lq, lk, gain, *, lam_init, tq, tk):
    bsz, seq, _ = dq.shape
    kernel = functools.partial(_attn_kernel, tq=tq, tk=tk, seq=seq, lam_init=lam_init)
    kv_spec = pl.BlockSpec((1, seq, DIFF_HV), lambda b, h, i: (b, 0, h))
    return pl.pallas_call(
        kernel,
        grid=(bsz, DIFF_HEADS, seq // tq),
        in_specs=[
            pl.BlockSpec((1, tq, DIFF_HV), lambda b, h, i: (b, i, h)),
            kv_spec, kv_spec,
            pl.BlockSpec((2, DIFF_HEAD_DIM), lambda b, h, i: (0, 0)),
            pl.BlockSpec((2, DIFF_HEAD_DIM), lambda b, h, i: (0, 0)),
            pl.BlockSpec((1, DIFF_HV), lambda b, h, i: (0, 0)),
        ],
        out_specs=pl.BlockSpec((1, tq, DIFF_HV), lambda b, h, i: (b, i, h)),
        out_shape=jax.ShapeDtypeStruct((bsz, seq, D_MODEL), BF16),
        scratch_shapes=[
            pltpu.VMEM((2 * tq, DIFF_HV), BF16),
            pltpu.VMEM((2 * tq, LANES), F32),
            pltpu.VMEM((2 * tq, LANES), F32),
            pltpu.VMEM((2 * tq, DIFF_HV), F32),
        ],
        compiler_params=pltpu.CompilerParams(
            dimension_semantics=("parallel", "parallel", "arbitrary"),
            vmem_limit_bytes=VMEM_LIMIT_BYTES),
        name="diff_attn",
    )(dq, dk, dv, lq, lk, gain)


def _out_kernel(x_ref, gate_ref, of_ref, ob_ref, z_ref, od_ref, dz_ref, mg_ref, md_ref,
                ggain_ref, fgain_ref, wbg_ref, wbd_ref, wo_ref, y_ref):
    og = of_ref[0].astype(F32) + ob_ref[0].astype(F32)
    ggain = ggain_ref[...]
    heads = []
    for h in range(GLA_HEADS):
        oh = og[:, h * GLA_HV:(h + 1) * GLA_HV]
        ms = jnp.mean(oh * oh, axis=1, keepdims=True)
        heads.append(oh * lax.rsqrt(ms + EPS) * ggain)
    og = jnp.concatenate(heads, axis=1) * z_ref[0].astype(F32)
    y_gla = jnp.dot(og.astype(BF16), wbg_ref[...], preferred_element_type=F32)
    od = od_ref[0].astype(F32) * dz_ref[0].astype(F32)
    y_diff = jnp.dot(od.astype(BF16), wbd_ref[...], preferred_element_type=F32)
    merged = mg_ref[0].astype(F32) * y_gla + md_ref[0].astype(F32) * y_diff
    y = jnp.dot(merged.astype(BF16), wo_ref[...], preferred_element_type=F32)
    r = x_ref[0] + gate_ref[0] * y
    ms = jnp.mean(r * r, axis=1, keepdims=True)
    y_ref[0] = r * lax.rsqrt(ms + EPS) * fgain_ref[...]


def _out_stage(x, gate, o_f, o_b, z, o_d, dz, mg, md, ggain, fgain, wbg, wbd, wo, *, tile):
    bsz, seq, _ = x.shape
    row = pl.BlockSpec((1, tile, D_MODEL), lambda b, i: (b, i, 0))
    per_b = pl.BlockSpec((1, 1, D_MODEL), lambda b, i: (b, 0, 0))
    return pl.pallas_call(
        _out_kernel,
        grid=(bsz, seq // tile),
        in_specs=[row, per_b] + [row] * 7 + [
            _const_spec(ggain.shape), _const_spec(fgain.shape),
            _const_spec(wbg.shape), _const_spec(wbd.shape), _const_spec(wo.shape)],
        out_specs=row,
        out_shape=jax.ShapeDtypeStruct((bsz, seq, D_MODEL), F32),
        compiler_params=pltpu.CompilerParams(
            dimension_semantics=("parallel", "parallel"), vmem_limit_bytes=VMEM_LIMIT_BYTES),
        name="out_stage",
    )(x, gate, o_f, o_b, z, o_d, dz, mg, md, ggain, fgain, wbg, wbd, wo)


def _rope_tables(seq):
    half = DIFF_HEAD_DIM // 2
    inv = 1.0 / (ROPE_THETA ** (jnp.arange(half, dtype=F32) / half))
    ang = jnp.arange(seq, dtype=F32)[:, None] * inv[None, :]
    cos = jnp.cos(ang)
    sin = jnp.sin(ang)
    cs = jnp.concatenate([cos, cos, cos, cos], axis=1)
    sn = jnp.concatenate([-sin, sin, -sin, sin], axis=1)
    return cs, sn


def _layer_weights(w_in, w_alpha, b_alpha):
    splits = (GLA_DK, GLA_DK, GLA_DV, GLA_DV, 2 * GLA_RANK) + (D_MODEL,) * 6
    pts = np.cumsum(splits)[:-1].tolist()
    a_q, a_k, a_v, a_z, a_low, d_q, d_k, d_v, d_z, m_g, m_d = jnp.split(w_in, pts, axis=1)
    wg = jnp.concatenate([a_q, a_k, a_v, a_z], axis=1).astype(BF16)
    zeros = jnp.zeros((GLA_RANK, GLA_DK), w_alpha.dtype)
    walpha = jnp.concatenate([
        jnp.concatenate([w_alpha[0], zeros], axis=1),
        jnp.concatenate([zeros, w_alpha[1]], axis=1)], axis=0).astype(BF16)
    balpha = b_alpha.reshape(1, 2 * GLA_DK)
    return (wg, a_low.astype(BF16), walpha, balpha,
            d_q.astype(BF16), d_k.astype(BF16), d_v.astype(BF16), d_z.astype(BF16),
            m_g.astype(BF16), m_d.astype(BF16))


def _trunk_layer(x, mod, lam_init, norm_gain, in_weights, gla_norm_gain, lambda_q, lambda_k,
                 diff_norm_gain, wbg, wbd, wo, final_gain, *, row_tile, gla_tile, tq, tk):
    bsz, seq, _ = x.shape
    row_tile, gla_tile, tq, tk = (min(t, seq) for t in (row_tile, gla_tile, tq, tk))
    shift, scale, gate = [m.reshape(bsz, 1, D_MODEL) for m in jnp.split(mod, 3, axis=-1)]
    cs, sn = _rope_tables(seq)
    (q, k, v, z, bcum, dq, dk, dv, dz, mg, md) = _inproj(
        x, shift, scale, norm_gain.reshape(1, D_MODEL), cs, sn, in_weights, row_tile)
    o_f = _gla(q, k, v, bcum, reverse=False, tile=gla_tile)
    o_b = _gla(q, k, v, bcum, reverse=True, tile=gla_tile)
    o_d = _attn(dq, dk, dv, lambda_q, lambda_k, diff_norm_gain.reshape(1, DIFF_HV),
                lam_init=lam_init, tq=tq, tk=tk)
    return _out_stage(x, gate, o_f, o_b, z, o_d, dz, mg, md,
                      gla_norm_gain.reshape(1, GLA_HV), final_gain.reshape(1, D_MODEL),
                      wbg, wbd, wo, tile=row_tile)


def kernel(x_prompt, x_sample, c_prompt, c_sample, w_ada, b_ada, norm_gain, w_in, w_alpha, b_alpha,
           gla_norm_gain, lambda_q, lambda_k, diff_norm_gain, w_bo_gla, w_bo_diff, w_out, final_gain):
    depth = w_in.shape[0]
    assert depth == 1, "final RMSNorm is fused into the (single) layer's output stage"
    layer = 0
    lam_init = 0.8 - 0.6 * math.exp(-0.3 * layer)
    nb_p, nb_s = c_prompt.shape[0], c_sample.shape[0]
    rows = -(-(nb_p + nb_s) // 8) * 8
    c_all = jnp.concatenate(
        [c_prompt, c_sample, jnp.zeros((rows - nb_p - nb_s, D_MODEL), F32)], axis=0)
    mod = _ada(c_all, w_ada[layer].astype(BF16), b_ada[layer].reshape(1, 3 * D_MODEL))
    in_weights = _layer_weights(w_in[layer], w_alpha[layer], b_alpha[layer])
    common = dict(
        lam_init=lam_init, norm_gain=norm_gain[layer], in_weights=in_weights,
        gla_norm_gain=gla_norm_gain[layer], lambda_q=lambda_q[layer], lambda_k=lambda_k[layer],
        diff_norm_gain=diff_norm_gain[layer], wbg=w_bo_gla[layer].astype(BF16),
        wbd=w_bo_diff[layer].astype(BF16), wo=w_out[layer].astype(BF16), final_gain=final_gain)
    y_prompt = _trunk_layer(x_prompt, mod[:nb_p], row_tile=256, gla_tile=512, tq=256, tk=512, **common)
    y_sample = _trunk_layer(x_sample, mod[nb_p:nb_p + nb_s], row_tile=256, gla_tile=512, tq=256, tk=512,
                            **common)
    return (y_prompt, y_sample)
```

```python
import functools
import math

import numpy as np
import jax
import jax.numpy as jnp
from jax import lax
from jax.experimental import pallas as pl
from jax.experimental.pallas import tpu as pltpu

D_MODEL = 1024
GLA_HEADS = 4
GLA_DK = D_MODEL // 2
GLA_DV = D_MODEL
GLA_HK = GLA_DK // GLA_HEADS
GLA_HV = GLA_DV // GLA_HEADS
GLA_RANK = 16
GLA_LOGIT_NORM = 16.0
GLA_CHUNK = 64
GLA_SUB = 8
DIFF_HEAD_DIM = 64
DIFF_HEADS = D_MODEL // (2 * DIFF_HEAD_DIM)
DIFF_HV = 2 * DIFF_HEAD_DIM
ROPE_THETA = 10000.0
EPS = 1e-6
LANES = 128
ATTN_Q_SCALE = (DIFF_HEAD_DIM ** -0.5) * math.log2(math.e)

F32 = jnp.float32
BF16 = jnp.bfloat16

VMEM_LIMIT_BYTES = 56 * 1024 * 1024

NT_DIMS = (((1,), (1,)), ((), ()))
TN_DIMS = (((0,), (0,)), ((), ()))


def _sigmoid(x):
    return 1.0 / (1.0 + jnp.exp(-x))


def _const_spec(shape):
    nd = len(shape)
    return pl.BlockSpec(shape, lambda *_: (0,) * nd, pipeline_mode=pl.Buffered(1))


def _ada_kernel(c_ref, w_ref, b_ref, o_ref):
    c = c_ref[...]
    s = c * _sigmoid(c)
    o_ref[...] = jnp.dot(s.astype(BF16), w_ref[...], preferred_element_type=F32) + b_ref[...]


def _ada(c, w_bf16, b):
    rows = c.shape[0]
    n_out = w_bf16.shape[1]
    tn = D_MODEL
    return pl.pallas_call(
        _ada_kernel,
        grid=(n_out // tn,),
        in_specs=[
            pl.BlockSpec((rows, D_MODEL), lambda j: (0, 0)),
            pl.BlockSpec((D_MODEL, tn), lambda j: (0, j)),
            pl.BlockSpec((1, tn), lambda j: (0, j)),
        ],
        out_specs=pl.BlockSpec((rows, tn), lambda j: (0, j)),
        out_shape=jax.ShapeDtypeStruct((rows, n_out), F32),
        name="ada",
    )(c, w_bf16, b)


def _rope_slices(x, cs, sn):
    lane = lax.broadcasted_iota(jnp.int32, (x.shape[0], LANES), 1)
    is_x2 = (lane & 32) != 0
    outs = []
    for s in range(x.shape[1] // LANES):
        xs = x[:, s * LANES:(s + 1) * LANES]
        from_lo = pltpu.roll(xs, 32, axis=1)
        from_hi = pltpu.roll(xs, LANES - 32, axis=1)
        partner = jnp.where(is_x2, from_lo, from_hi)
        outs.append(xs * cs + partner * sn)
    return jnp.concatenate(outs, axis=1)


def _inproj_kernel(x_ref, shift_ref, scale_ref, gain_ref, cs_ref, sn_ref, trif_ref, trib_ref,
                   wg_ref, wlow_ref, walpha_ref, balpha_ref,
                   wdq_ref, wdk_ref, wdv_ref, wdz_ref, wmg_ref, wmd_ref,
                   q_out, k_out, v_out, z_out, b_out,
                   dq_out, dk_out, dv_out, dz_out, mg_out, md_out):
    x = x_ref[0]
    ms = jnp.mean(x * x, axis=-1, keepdims=True)
    y = x * lax.rsqrt(ms + EPS) * gain_ref[...]
    h = (y * (1.0 + scale_ref[0]) + shift_ref[0]).astype(BF16)

    pg = jnp.dot(h, wg_ref[...], preferred_element_type=F32)
    q_out[0] = (pg[:, 0:GLA_DK] * (GLA_HK ** -0.5)).astype(BF16)
    k_out[0] = pg[:, GLA_DK:2 * GLA_DK].astype(BF16)
    v_out[0] = pg[:, 2 * GLA_DK:2 * GLA_DK + GLA_DV].astype(BF16)
    az = pg[:, 2 * GLA_DK + GLA_DV:]
    z_out[0] = (az * _sigmoid(az)).astype(BF16)

    low = jnp.dot(h, wlow_ref[...], preferred_element_type=F32)
    logits = jnp.dot(low.astype(BF16), walpha_ref[...], preferred_element_type=F32) + balpha_ref[...]
    g = (jnp.minimum(logits, 0.0) - jnp.log1p(jnp.exp(-jnp.abs(logits)))) * (1.0 / GLA_LOGIT_NORM)
    g_hi = g.astype(BF16)
    g_lo = (g - g_hi.astype(F32)).astype(BF16)
    for tri_ref, lo in ((trif_ref, 0), (trib_ref, GLA_DK)):
        tri = tri_ref[...]
        b_out[0, :, lo:lo + GLA_DK] = (
            jnp.dot(tri, g_hi[:, lo:lo + GLA_DK], preferred_element_type=F32)
            + jnp.dot(tri, g_lo[:, lo:lo + GLA_DK], preferred_element_type=F32))

    cs = cs_ref[...]
    sn = sn_ref[...]
    dq = jnp.dot(h, wdq_ref[...], preferred_element_type=F32)
    dq_out[0] = (_rope_slices(dq, cs, sn) * ATTN_Q_SCALE).astype(BF16)
    dk = jnp.dot(h, wdk_ref[...], preferred_element_type=F32)
    dk_out[0] = _rope_slices(dk, cs, sn).astype(BF16)
    dv_out[0] = lax.dot_general(wdv_ref[...], h, NT_DIMS, preferred_element_type=F32).astype(BF16)
    dz = jnp.dot(h, wdz_ref[...], preferred_element_type=F32)
    dz_out[0] = (dz * _sigmoid(dz)).astype(BF16)
    mg_out[0] = _sigmoid(jnp.dot(h, wmg_ref[...], preferred_element_type=F32)).astype(BF16)
    md_out[0] = _sigmoid(jnp.dot(h, wmd_ref[...], preferred_element_type=F32)).astype(BF16)


def _chunk_tri(tile, reverse):
    i = np.arange(tile)[:, None]
    j = np.arange(tile)[None, :]
    same = (i // GLA_CHUNK) == (j // GLA_CHUNK)
    tri = (j >= i) if reverse else (j <= i)
    return jnp.asarray(same & tri, dtype=BF16)


def _inproj(x, shift, scale, gain, cs, sn, weights, tile):
    bsz, seq, _ = x.shape
    (wg, wlow, walpha, balpha, wdq, wdk, wdv, wdz, wmg, wmd) = weights
    trif = _chunk_tri(tile, False)
    trib = _chunk_tri(tile, True)
    row = lambda c: pl.BlockSpec((1, tile, c), lambda b, i: (b, i, 0))
    per_b = pl.BlockSpec((1, 1, D_MODEL), lambda b, i: (b, 0, 0))
    consts = [gain, None, None, trif, trib, wg, wlow, walpha, balpha, wdq, wdk, wdv, wdz, wmg, wmd]
    in_specs = [row(D_MODEL), per_b, per_b]
    for a in consts:
        if a is None:
            in_specs.append(pl.BlockSpec((tile, LANES), lambda b, i: (i, 0)))
        else:
            in_specs.append(_const_spec(a.shape))
    out_cols = [(GLA_DK, BF16), (GLA_DK, BF16), (GLA_DV, BF16), (GLA_DV, BF16), (2 * GLA_DK, F32)] + \
               [(D_MODEL, BF16)] * 6
    out_specs = [row(c) for c, _ in out_cols]
    out_shape = [jax.ShapeDtypeStruct((bsz, seq, c), dt) for c, dt in out_cols]
    dv_index = 7
    out_specs[dv_index] = pl.BlockSpec((1, D_MODEL, tile), lambda b, i: (b, 0, i))
    out_shape[dv_index] = jax.ShapeDtypeStruct((bsz, D_MODEL, seq), BF16)
    return pl.pallas_call(
        _inproj_kernel,
        grid=(bsz, seq // tile),
        in_specs=in_specs,
        out_specs=out_specs,
        out_shape=out_shape,
        compiler_params=pltpu.CompilerParams(
            dimension_semantics=("parallel", "parallel"), vmem_limit_bytes=VMEM_LIMIT_BYTES),
        name="inproj",
    )(x, shift, scale, gain, cs, sn, trif, trib, wg, wlow, walpha, balpha, wdq, wdk, wdv, wdz, wmg, wmd)


def _gla_masks(reverse):
    c = GLA_CHUNK
    i = np.arange(c)[:, None]
    j = np.arange(c)[None, :]
    q_half, k_half = (0, 1) if reverse else (1, 0)
    levels = []
    for s in (8, 16, 32):
        same_parent = (i // (2 * s)) == (j // (2 * s))
        levels.append(same_parent & ((i // s) % 2 == q_half) & ((j // s) % 2 == k_half))
    same_blk = (i // GLA_SUB) == (j // GLA_SUB)
    tri = (j >= i) if reverse else (j <= i)
    masks = levels + [same_blk & tri]
    return jnp.asarray(np.stack(masks).astype(np.float32))


def _gla_kernel(q_ref, k_ref, v_ref, b_ref, mask_ref, o_ref, state_ref, k32_ref, b32_ref, *,
                reverse, tile):
    c = GLA_CHUNK
    n_chunks = tile // c

    @pl.when(pl.program_id(1) == 0)
    def _():
        state_ref[...] = jnp.zeros_like(state_ref)

    lane64 = lax.broadcasted_iota(jnp.int32, (GLA_SUB, c), 1)
    scan_last = 0 if reverse else c - 1

    def chunk_body(ci, carry):
        cc = (n_chunks - 1 - ci) if reverse else ci
        row0 = pl.multiple_of(cc * c, c)
        rows = pl.ds(row0, c)
        for h in range(GLA_HEADS):
            lk = slice(h * GLA_HK, (h + 1) * GLA_HK)
            lv = slice(h * GLA_HV, (h + 1) * GLA_HV)
            q = q_ref[0, rows, lk].astype(F32)
            k = k_ref[0, rows, lk].astype(F32)
            v = v_ref[0, rows, lv]
            b = b_ref[0, rows, lk]
            k32_ref[h] = k
            b32_ref[h] = b

            def b_row(r):
                return b32_ref[h, pl.ds(r, 1), :]

            b_tot = b_row(scan_last)

            state = state_ref[h]
            q_in = (q * jnp.exp(b)).astype(BF16)
            o = jnp.dot(q_in, state.astype(BF16), preferred_element_type=F32)

            scores = jnp.zeros((c, c), F32)
            for lvl, s in enumerate((8, 16, 32)):
                pieces = []
                for p in range(c // (2 * s)):
                    m = 2 * s * p + (s if reverse else s - 1)
                    pieces.append(jnp.broadcast_to(b_row(m), (2 * s, GLA_HK)))
                ref = pieces[0] if len(pieces) == 1 else jnp.concatenate(pieces, axis=0)
                qt = (q * jnp.exp(jnp.minimum(b - ref, 0.0))).astype(BF16)
                kt = (k * jnp.exp(jnp.minimum(ref - b, 0.0))).astype(BF16)
                scores = scores + mask_ref[lvl] * lax.dot_general(
                    qt, kt, NT_DIMS, preferred_element_type=F32)

            blocks = []
            for jb in range(c // GLA_SUB):
                r0 = jb * GLA_SUB
                qb = q[r0:r0 + GLA_SUB]
                bb = b[r0:r0 + GLA_SUB]
                acc = jnp.zeros((GLA_SUB, c), F32)
                for jl in range(GLA_SUB):
                    col = r0 + jl
                    k_row = k32_ref[h, pl.ds(col, 1), :]
                    w = qb * k_row * jnp.exp(jnp.minimum(bb - b_row(col), 0.0))
                    acc = jnp.where(lane64 == col, jnp.sum(w, axis=1, keepdims=True), acc)
                blocks.append(acc)
            exact = jnp.concatenate(blocks, axis=0)
            scores = scores + mask_ref[3] * exact

            o = o + jnp.dot(scores.astype(BF16), v, preferred_element_type=F32)
            o_ref[0, rows, lv] = o.astype(o_ref.dtype)

            k_hat = (k * jnp.exp(b_tot - b)).astype(BF16)
            delta = lax.dot_general(k_hat, v, TN_DIMS, preferred_element_type=F32)
            decay_col = jnp.transpose(jnp.broadcast_to(jnp.exp(b_tot), (GLA_HK, GLA_HK)))
            decay = jnp.concatenate([decay_col] * (GLA_HV // GLA_HK), axis=1)
            state_ref[h] = state * decay + delta
        return carry

    lax.fori_loop(0, n_chunks, chunk_body, 0)


def _gla(q, k, v, bcum, *, reverse, tile):
    bsz, seq, _ = q.shape
    nt = seq // tile
    masks = _gla_masks(reverse)
    blk = (lambda b, t: (b, nt - 1 - t, 0)) if reverse else (lambda b, t: (b, t, 0))
    bcol = 1 if reverse else 0
    bblk = (lambda b, t: (b, nt - 1 - t, bcol)) if reverse else (lambda b, t: (b, t, bcol))
    kernel = functools.partial(_gla_kernel, reverse=reverse, tile=tile)
    return pl.pallas_call(
        kernel,
        grid=(bsz, nt),
        in_specs=[
            pl.BlockSpec((1, tile, GLA_DK), blk),
            pl.BlockSpec((1, tile, GLA_DK), blk),
            pl.BlockSpec((1, tile, GLA_DV), blk),
            pl.BlockSpec((1, tile, GLA_DK), bblk),
            _const_spec(masks.shape),
        ],
        out_specs=pl.BlockSpec((1, tile, GLA_DV), blk),
        out_shape=jax.ShapeDtypeStruct((bsz, seq, GLA_DV), BF16),
        scratch_shapes=[
            pltpu.VMEM((GLA_HEADS, GLA_HK, GLA_HV), F32),
            pltpu.VMEM((GLA_HEADS, GLA_CHUNK, GLA_HK), F32),
            pltpu.VMEM((GLA_HEADS, GLA_CHUNK, GLA_HK), F32),
        ],
        compiler_params=pltpu.CompilerParams(
            dimension_semantics=("parallel", "arbitrary"), vmem_limit_bytes=VMEM_LIMIT_BYTES),
        name="gla_bwd" if reverse else "gla_fwd",
    )(q, k, v, bcum, masks)


def _attn_kernel(q_ref, k_ref, vt_ref, lq_ref, lk_ref, gain_ref, o_ref,
                 q2_ref, s_ref, smax_ref, m_ref, l_ref, acc_ref, *, tq, tk, seq, lam_init):
    q = q_ref[0]
    lane = lax.broadcasted_iota(jnp.int32, (tq, LANES), 1)
    zero = jnp.zeros_like(q)
    q2_ref[0:tq] = jnp.where(lane < DIFF_HEAD_DIM, q, zero)
    q2_ref[tq:2 * tq] = jnp.where(lane >= DIFF_HEAD_DIM, q, zero)
    m_ref[...] = jnp.full_like(m_ref, -jnp.inf)
    l_ref[...] = jnp.zeros_like(l_ref)
    acc_ref[...] = jnp.zeros_like(acc_ref)
    n_tiles = seq // tk

    def scores(kt, slot):
        ks = pl.multiple_of(kt * tk, tk)
        k = k_ref[0, pl.ds(ks, tk), :]
        s = lax.dot_general(k, q2_ref[...], NT_DIMS, preferred_element_type=F32)
        s_ref[slot] = s
        smax_ref[slot] = jnp.max(s, axis=0, keepdims=True)

    def softmax_values(kt, slot):
        ks = pl.multiple_of(kt * tk, tk)
        vt = vt_ref[0, :, pl.ds(ks, tk)]
        s = s_ref[slot]
        m_prev = m_ref[...]
        m_new = jnp.maximum(m_prev, smax_ref[slot])
        alpha = jnp.exp2(m_prev - m_new)
        p = jnp.exp2(s - m_new)
        l_ref[...] = alpha * l_ref[...] + jnp.sum(p, axis=0, keepdims=True)
        m_ref[...] = m_new
        acc_ref[...] = alpha * acc_ref[...] + jnp.dot(vt, p.astype(BF16), preferred_element_type=F32)

    def step(i, slot):
        scores(i + 1, 1 - slot)
        softmax_values(i, slot)

    scores(0, 0)

    def pair(j, carry):
        i = 2 * j
        step(i, 0)
        step(i + 1, 1)
        return carry

    lax.fori_loop(0, (n_tiles - 2) // 2, pair, 0)
    step(n_tiles - 2, 0)
    softmax_values(n_tiles - 1, 1)

    o2 = acc_ref[...] / l_ref[...]
    dots = jnp.sum(lq_ref[...] * lk_ref[...], axis=1, keepdims=True)
    e = jnp.exp(dots)
    lam = e[0:1, :] - e[1:2, :] + lam_init
    ot = o2[:, 0:tq] - lam * o2[:, tq:2 * tq]
    ms = jnp.mean(ot * ot, axis=0, keepdims=True)
    on = jnp.transpose(ot * lax.rsqrt(ms + EPS))
    o_ref[0] = (on * gain_ref[...] * (1.0 - lam_init)).astype(o_ref.dtype)


def _attn(dq, dk, dvt, lq, lk, gain, *, lam_init, tq, tk):
    bsz, seq, _ = dq.shape
    assert seq % (2 * tk) == 0
    kernel = functools.partial(_attn_kernel, tq=tq, tk=tk, seq=seq, lam_init=lam_init)
    return pl.pallas_call(
        kernel,
        grid=(bsz, DIFF_HEADS, seq // tq),
        in_specs=[
            pl.BlockSpec((1, tq, DIFF_HV), lambda b, h, i: (b, i, h)),
            pl.BlockSpec((1, seq, DIFF_HV), lambda b, h, i: (b, 0, h)),
            pl.BlockSpec((1, DIFF_HV, seq), lambda b, h, i: (b, h, 0)),
            pl.BlockSpec((2, DIFF_HEAD_DIM), lambda b, h, i: (0, 0)),
            pl.BlockSpec((2, DIFF_HEAD_DIM), lambda b, h, i: (0, 0)),
            pl.BlockSpec((1, DIFF_HV), lambda b, h, i: (0, 0)),
        ],
        out_specs=pl.BlockSpec((1, tq, DIFF_HV), lambda b, h, i: (b, i, h)),
        out_shape=jax.ShapeDtypeStruct((bsz, seq, D_MODEL), BF16),
        scratch_shapes=[
            pltpu.VMEM((2 * tq, DIFF_HV), BF16),
            pltpu.VMEM((2, tk, 2 * tq), F32),
            pltpu.VMEM((2, 1, 2 * tq), F32),
            pltpu.VMEM((1, 2 * tq), F32),
            pltpu.VMEM((1, 2 * tq), F32),
            pltpu.VMEM((DIFF_HV, 2 * tq), F32),
        ],
        compiler_params=pltpu.CompilerParams(
            dimension_semantics=("parallel", "parallel", "arbitrary"),
            vmem_limit_bytes=VMEM_LIMIT_BYTES),
        name="diff_attn",
    )(dq, dk, dvt, lq, lk, gain)


def _out_kernel(x_ref, gate_ref, of_ref, ob_ref, z_ref, od_ref, dz_ref, mg_ref, md_ref,
                ggain_ref, fgain_ref, wbg_ref, wbd_ref, wo_ref, y_ref):
    og = of_ref[0].astype(F32) + ob_ref[0].astype(F32)
    ggain = ggain_ref[...]
    heads = []
    for h in range(GLA_HEADS):
        oh = og[:, h * GLA_HV:(h + 1) * GLA_HV]
        ms = jnp.mean(oh * oh, axis=1, keepdims=True)
        heads.append(oh * lax.rsqrt(ms + EPS) * ggain)
    og = jnp.concatenate(heads, axis=1) * z_ref[0].astype(F32)
    y_gla = jnp.dot(og.astype(BF16), wbg_ref[...], preferred_element_type=F32)
    od = od_ref[0].astype(F32) * dz_ref[0].astype(F32)
    y_diff = jnp.dot(od.astype(BF16), wbd_ref[...], preferred_element_type=F32)
    merged = mg_ref[0].astype(F32) * y_gla + md_ref[0].astype(F32) * y_diff
    y = jnp.dot(merged.astype(BF16), wo_ref[...], preferred_element_type=F32)
    r = x_ref[0] + gate_ref[0] * y
    ms = jnp.mean(r * r, axis=1, keepdims=True)
    y_ref[0] = r * lax.rsqrt(ms + EPS) * fgain_ref[...]


def _out_stage(x, gate, o_f, o_b, z, o_d, dz, mg, md, ggain, fgain, wbg, wbd, wo, *, tile):
    bsz, seq, _ = x.shape
    row = pl.BlockSpec((1, tile, D_MODEL), lambda b, i: (b, i, 0))
    per_b = pl.BlockSpec((1, 1, D_MODEL), lambda b, i: (b, 0, 0))
    return pl.pallas_call(
        _out_kernel,
        grid=(bsz, seq // tile),
        in_specs=[row, per_b] + [row] * 7 + [
            _const_spec(ggain.shape), _const_spec(fgain.shape),
            _const_spec(wbg.shape), _const_spec(wbd.shape), _const_spec(wo.shape)],
        out_specs=row,
        out_shape=jax.ShapeDtypeStruct((bsz, seq, D_MODEL), F32),
        compiler_params=pltpu.CompilerParams(
            dimension_semantics=("parallel", "parallel"), vmem_limit_bytes=VMEM_LIMIT_BYTES),
        name="out_stage",
    )(x, gate, o_f, o_b, z, o_d, dz, mg, md, ggain, fgain, wbg, wbd, wo)


def _rope_tables(seq):
    half = DIFF_HEAD_DIM // 2
    inv = 1.0 / (ROPE_THETA ** (jnp.arange(half, dtype=F32) / half))
    ang = jnp.arange(seq, dtype=F32)[:, None] * inv[None, :]
    cos = jnp.cos(ang)
    sin = jnp.sin(ang)
    cs = jnp.concatenate([cos, cos, cos, cos], axis=1)
    sn = jnp.concatenate([-sin, sin, -sin, sin], axis=1)
    return cs, sn


def _layer_weights(w_in, w_alpha, b_alpha):
    splits = (GLA_DK, GLA_DK, GLA_DV, GLA_DV, 2 * GLA_RANK) + (D_MODEL,) * 6
    pts = np.cumsum(splits)[:-1].tolist()
    a_q, a_k, a_v, a_z, a_low, d_q, d_k, d_v, d_z, m_g, m_d = jnp.split(w_in, pts, axis=1)
    wg = jnp.concatenate([a_q, a_k, a_v, a_z], axis=1).astype(BF16)
    zeros = jnp.zeros((GLA_RANK, GLA_DK), w_alpha.dtype)
    walpha = jnp.concatenate([
        jnp.concatenate([w_alpha[0], zeros], axis=1),
        jnp.concatenate([zeros, w_alpha[1]], axis=1)], axis=0).astype(BF16)
    balpha = b_alpha.reshape(1, 2 * GLA_DK)
    return (wg, a_low.astype(BF16), walpha, balpha,
            d_q.astype(BF16), d_k.astype(BF16), d_v.T.astype(BF16), d_z.astype(BF16),
            m_g.astype(BF16), m_d.astype(BF16))


def _trunk_layer(x, mod, lam_init, norm_gain, in_weights, gla_norm_gain, lambda_q, lambda_k,
                 diff_norm_gain, wbg, wbd, wo, final_gain, *, row_tile, gla_tile, tq, tk):
    bsz, seq, _ = x.shape
    row_tile, gla_tile, tq, tk = (min(t, s) for t, s in
                                  ((row_tile, seq), (gla_tile, seq), (tq, seq), (tk, seq // 2)))
    shift, scale, gate = [m.reshape(bsz, 1, D_MODEL) for m in jnp.split(mod, 3, axis=-1)]
    cs, sn = _rope_tables(seq)
    (q, k, v, z, bcum, dq, dk, dv, dz, mg, md) = _inproj(
        x, shift, scale, norm_gain.reshape(1, D_MODEL), cs, sn, in_weights, row_tile)
    o_f = _gla(q, k, v, bcum, reverse=False, tile=gla_tile)
    o_b = _gla(q, k, v, bcum, reverse=True, tile=gla_tile)
    o_d = _attn(dq, dk, dv, lambda_q, lambda_k, diff_norm_gain.reshape(1, DIFF_HV),
                lam_init=lam_init, tq=tq, tk=tk)
    return _out_stage(x, gate, o_f, o_b, z, o_d, dz, mg, md,
                      gla_norm_gain.reshape(1, GLA_HV), final_gain.reshape(1, D_MODEL),
                      wbg, wbd, wo, tile=row_tile)


def kernel(x_prompt, x_sample, c_prompt, c_sample, w_ada, b_ada, norm_gain, w_in, w_alpha, b_alpha,
           gla_norm_gain, lambda_q, lambda_k, diff_norm_gain, w_bo_gla, w_bo_diff, w_out, final_gain):
    depth = w_in.shape[0]
    assert depth == 1, "final RMSNorm is fused into the (single) layer's output stage"
    layer = 0
    lam_init = 0.8 - 0.6 * math.exp(-0.3 * layer)
    nb_p, nb_s = c_prompt.shape[0], c_sample.shape[0]
    rows = -(-(nb_p + nb_s) // 8) * 8
    c_all = jnp.concatenate(
        [c_prompt, c_sample, jnp.zeros((rows - nb_p - nb_s, D_MODEL), F32)], axis=0)
    mod = _ada(c_all, w_ada[layer].astype(BF16), b_ada[layer].reshape(1, 3 * D_MODEL))
    in_weights = _layer_weights(w_in[layer], w_alpha[layer], b_alpha[layer])
    common = dict(
        lam_init=lam_init, norm_gain=norm_gain[layer], in_weights=in_weights,
        gla_norm_gain=gla_norm_gain[layer], lambda_q=lambda_q[layer], lambda_k=lambda_k[layer],
        diff_norm_gain=diff_norm_gain[layer], wbg=w_bo_gla[layer].astype(BF16),
        wbd=w_bo_diff[layer].astype(BF16), wo=w_out[layer].astype(BF16), final_gain=final_gain)
    y_prompt = _trunk_layer(x_prompt, mod[:nb_p], row_tile=256, gla_tile=512, tq=512, tk=1024, **common)
    y_sample = _trunk_layer(x_sample, mod[nb_p:nb_p + nb_s], row_tile=256, gla_tile=512, tq=512, tk=1024,
                            **common)
    return (y_prompt, y_sample)
```

```python
import functools
import math

import numpy as np
import jax
import jax.numpy as jnp
from jax import lax
from jax.experimental import pallas as pl
from jax.experimental.pallas import tpu as pltpu

D_MODEL = 1024
GLA_HEADS = 4
GLA_DK = D_MODEL // 2
GLA_DV = D_MODEL
GLA_HK = GLA_DK // GLA_HEADS
GLA_HV = GLA_DV // GLA_HEADS
GLA_RANK = 16
GLA_LOGIT_NORM = 16.0
GLA_CHUNK = 64
GLA_SUB = 8
DIFF_HEAD_DIM = 64
DIFF_HEADS = D_MODEL // (2 * DIFF_HEAD_DIM)
DIFF_HV = 2 * DIFF_HEAD_DIM
ROPE_THETA = 10000.0
EPS = 1e-6
LANES = 128
ONES_ROWS = 16
MXU_DEPTH = 256
ATTN_Q_SCALE = (DIFF_HEAD_DIM ** -0.5) * math.log2(math.e)

F32 = jnp.float32
BF16 = jnp.bfloat16

VMEM_LIMIT_BYTES = 56 * 1024 * 1024

NT_DIMS = (((1,), (1,)), ((), ()))
TN_DIMS = (((0,), (0,)), ((), ()))


def _sigmoid(x):
    return 1.0 / (1.0 + jnp.exp(-x))


def _const_spec(shape):
    nd = len(shape)
    return pl.BlockSpec(shape, lambda *_: (0,) * nd, pipeline_mode=pl.Buffered(1))


def _ada_kernel(c_ref, w_ref, b_ref, o_ref):
    c = c_ref[...]
    s = c * _sigmoid(c)
    o_ref[...] = jnp.dot(s.astype(BF16), w_ref[...], preferred_element_type=F32) + b_ref[...]


def _ada(c, w_bf16, b):
    rows = c.shape[0]
    n_out = w_bf16.shape[1]
    tn = D_MODEL
    return pl.pallas_call(
        _ada_kernel,
        grid=(n_out // tn,),
        in_specs=[
            pl.BlockSpec((rows, D_MODEL), lambda j: (0, 0)),
            pl.BlockSpec((D_MODEL, tn), lambda j: (0, j)),
            pl.BlockSpec((1, tn), lambda j: (0, j)),
        ],
        out_specs=pl.BlockSpec((rows, tn), lambda j: (0, j)),
        out_shape=jax.ShapeDtypeStruct((rows, n_out), F32),
        name="ada",
    )(c, w_bf16, b)


def _rope_slices(x, cs, sn):
    lane = lax.broadcasted_iota(jnp.int32, (x.shape[0], LANES), 1)
    is_x2 = (lane & 32) != 0
    outs = []
    for s in range(x.shape[1] // LANES):
        xs = x[:, s * LANES:(s + 1) * LANES]
        from_lo = pltpu.roll(xs, 32, axis=1)
        from_hi = pltpu.roll(xs, LANES - 32, axis=1)
        partner = jnp.where(is_x2, from_lo, from_hi)
        outs.append(xs * cs + partner * sn)
    return jnp.concatenate(outs, axis=1)


def _inproj_kernel(x_ref, shift_ref, scale_ref, gain_ref, cs_ref, sn_ref, trif_ref, trib_ref,
                   wg_ref, wlow_ref, walpha_ref, balpha_ref,
                   wdq_ref, wdk_ref, wdv_ref, wdz_ref, wmg_ref, wmd_ref,
                   q_out, k_out, v_out, z_out, b_out,
                   dq_out, dk_out, dv_out, dz_out, mg_out, md_out):
    x = x_ref[0]
    ms = jnp.mean(x * x, axis=-1, keepdims=True)
    y = x * lax.rsqrt(ms + EPS) * gain_ref[...]
    h = (y * (1.0 + scale_ref[0]) + shift_ref[0]).astype(BF16)

    pg = jnp.dot(h, wg_ref[...], preferred_element_type=F32)
    q_out[0] = (pg[:, 0:GLA_DK] * (GLA_HK ** -0.5)).astype(BF16)
    k_out[0] = pg[:, GLA_DK:2 * GLA_DK].astype(BF16)
    v_out[0] = pg[:, 2 * GLA_DK:2 * GLA_DK + GLA_DV].astype(BF16)
    az = pg[:, 2 * GLA_DK + GLA_DV:]
    z_out[0] = (az * _sigmoid(az)).astype(BF16)

    low = jnp.dot(h, wlow_ref[...], preferred_element_type=F32)
    logits = jnp.dot(low.astype(BF16), walpha_ref[...], preferred_element_type=F32) + balpha_ref[...]
    g = (jnp.minimum(logits, 0.0) - jnp.log1p(jnp.exp(-jnp.abs(logits)))) * (1.0 / GLA_LOGIT_NORM)
    g_hi = g.astype(BF16)
    g_lo = (g - g_hi.astype(F32)).astype(BF16)
    for tri_ref, lo in ((trif_ref, 0), (trib_ref, GLA_DK)):
        tri = tri_ref[...]
        b_out[0, :, lo:lo + GLA_DK] = (
            jnp.dot(tri, g_hi[:, lo:lo + GLA_DK], preferred_element_type=F32)
            + jnp.dot(tri, g_lo[:, lo:lo + GLA_DK], preferred_element_type=F32))

    cs = cs_ref[...]
    sn = sn_ref[...]
    dq = jnp.dot(h, wdq_ref[...], preferred_element_type=F32)
    dq_out[0] = (_rope_slices(dq, cs, sn) * ATTN_Q_SCALE).astype(BF16)
    dk = jnp.dot(h, wdk_ref[...], preferred_element_type=F32)
    dk_out[0] = _rope_slices(dk, cs, sn).astype(BF16)
    dv_out[0] = lax.dot_general(wdv_ref[...], h, NT_DIMS, preferred_element_type=F32).astype(BF16)
    dz = jnp.dot(h, wdz_ref[...], preferred_element_type=F32)
    dz_out[0] = (dz * _sigmoid(dz)).astype(BF16)
    mg_out[0] = _sigmoid(jnp.dot(h, wmg_ref[...], preferred_element_type=F32)).astype(BF16)
    md_out[0] = _sigmoid(jnp.dot(h, wmd_ref[...], preferred_element_type=F32)).astype(BF16)


def _chunk_tri(tile, reverse):
    i = np.arange(tile)[:, None]
    j = np.arange(tile)[None, :]
    same = (i // GLA_CHUNK) == (j // GLA_CHUNK)
    tri = (j >= i) if reverse else (j <= i)
    return jnp.asarray(same & tri, dtype=BF16)


def _inproj(x, shift, scale, gain, cs, sn, weights, tile):
    bsz, seq, _ = x.shape
    (wg, wlow, walpha, balpha, wdq, wdk, wdv, wdz, wmg, wmd) = weights
    trif = _chunk_tri(tile, False)
    trib = _chunk_tri(tile, True)
    row = lambda c: pl.BlockSpec((1, tile, c), lambda b, i: (b, i, 0))
    per_b = pl.BlockSpec((1, 1, D_MODEL), lambda b, i: (b, 0, 0))
    consts = [gain, None, None, trif, trib, wg, wlow, walpha, balpha, wdq, wdk, wdv, wdz, wmg, wmd]
    in_specs = [row(D_MODEL), per_b, per_b]
    for a in consts:
        if a is None:
            in_specs.append(pl.BlockSpec((tile, LANES), lambda b, i: (i, 0)))
        else:
            in_specs.append(_const_spec(a.shape))
    out_cols = [(GLA_DK, BF16), (GLA_DK, BF16), (GLA_DV, BF16), (GLA_DV, BF16), (2 * GLA_DK, F32)] + \
               [(D_MODEL, BF16)] * 6
    out_specs = [row(c) for c, _ in out_cols]
    out_shape = [jax.ShapeDtypeStruct((bsz, seq, c), dt) for c, dt in out_cols]
    dv_index = 7
    out_specs[dv_index] = pl.BlockSpec((1, D_MODEL, tile), lambda b, i: (b, 0, i))
    out_shape[dv_index] = jax.ShapeDtypeStruct((bsz, D_MODEL, seq), BF16)
    return pl.pallas_call(
        _inproj_kernel,
        grid=(bsz, seq // tile),
        in_specs=in_specs,
        out_specs=out_specs,
        out_shape=out_shape,
        compiler_params=pltpu.CompilerParams(
            dimension_semantics=("parallel", "parallel"), vmem_limit_bytes=VMEM_LIMIT_BYTES),
        name="inproj",
    )(x, shift, scale, gain, cs, sn, trif, trib, wg, wlow, walpha, balpha, wdq, wdk, wdv, wdz, wmg, wmd)


def _gla_masks(reverse):
    c = GLA_CHUNK
    i = np.arange(c)[:, None]
    j = np.arange(c)[None, :]
    q_half, k_half = (0, 1) if reverse else (1, 0)
    levels = []
    for s in (8, 16, 32):
        same_parent = (i // (2 * s)) == (j // (2 * s))
        levels.append(same_parent & ((i // s) % 2 == q_half) & ((j // s) % 2 == k_half))
    same_blk = (i // GLA_SUB) == (j // GLA_SUB)
    tri = (j >= i) if reverse else (j <= i)
    masks = levels + [same_blk & tri]
    return jnp.asarray(np.stack(masks).astype(np.float32))


def _gla_kernel(q_ref, k_ref, v_ref, b_ref, mask_ref, o_ref, state_ref, k32_ref, b32_ref, *,
                reverse, tile):
    c = GLA_CHUNK
    n_chunks = tile // c

    @pl.when(pl.program_id(1) == 0)
    def _():
        state_ref[...] = jnp.zeros_like(state_ref)

    lane64 = lax.broadcasted_iota(jnp.int32, (GLA_SUB, c), 1)
    scan_last = 0 if reverse else c - 1

    def chunk_body(ci, carry):
        cc = (n_chunks - 1 - ci) if reverse else ci
        row0 = pl.multiple_of(cc * c, c)
        rows = pl.ds(row0, c)
        for h in range(GLA_HEADS):
            lk = slice(h * GLA_HK, (h + 1) * GLA_HK)
            lv = slice(h * GLA_HV, (h + 1) * GLA_HV)
            q = q_ref[0, rows, lk].astype(F32)
            k = k_ref[0, rows, lk].astype(F32)
            v = v_ref[0, rows, lv]
            b = b_ref[0, rows, lk]
            k32_ref[h] = k
            b32_ref[h] = b

            def b_row(r):
                return b32_ref[h, pl.ds(r, 1), :]

            b_tot = b_row(scan_last)

            state = state_ref[h]
            q_in = (q * jnp.exp(b)).astype(BF16)
            o = jnp.dot(q_in, state.astype(BF16), preferred_element_type=F32)

            scores = jnp.zeros((c, c), F32)
            for lvl, s in enumerate((8, 16, 32)):
                pieces = []
                for p in range(c // (2 * s)):
                    m = 2 * s * p + (s if reverse else s - 1)
                    pieces.append(jnp.broadcast_to(b_row(m), (2 * s, GLA_HK)))
                ref = pieces[0] if len(pieces) == 1 else jnp.concatenate(pieces, axis=0)
                qt = (q * jnp.exp(jnp.minimum(b - ref, 0.0))).astype(BF16)
                kt = (k * jnp.exp(jnp.minimum(ref - b, 0.0))).astype(BF16)
                scores = scores + mask_ref[lvl] * lax.dot_general(
                    qt, kt, NT_DIMS, preferred_element_type=F32)

            blocks = []
            for jb in range(c // GLA_SUB):
                r0 = jb * GLA_SUB
                qb = q[r0:r0 + GLA_SUB]
                bb = b[r0:r0 + GLA_SUB]
                acc = jnp.zeros((GLA_SUB, c), F32)
                for jl in range(GLA_SUB):
                    col = r0 + jl
                    k_row = k32_ref[h, pl.ds(col, 1), :]
                    w = qb * k_row * jnp.exp(bb - b_row(col))
                    acc = jnp.where(lane64 == col, jnp.sum(w, axis=1, keepdims=True), acc)
                blocks.append(acc)
            exact = jnp.concatenate(blocks, axis=0)
            scores = jnp.where(mask_ref[3] > 0.5, exact, scores)

            o = o + jnp.dot(scores.astype(BF16), v, preferred_element_type=F32)
            o_ref[0, rows, lv] = o.astype(o_ref.dtype)

            k_hat = (k * jnp.exp(b_tot - b)).astype(BF16)
            delta = lax.dot_general(k_hat, v, TN_DIMS, preferred_element_type=F32)
            decay_col = jnp.transpose(jnp.broadcast_to(jnp.exp(b_tot), (GLA_HK, GLA_HK)))
            decay = jnp.concatenate([decay_col] * (GLA_HV // GLA_HK), axis=1)
            state_ref[h] = state * decay + delta
        return carry

    lax.fori_loop(0, n_chunks, chunk_body, 0, unroll=2)


def _gla(q, k, v, bcum, *, reverse, tile):
    bsz, seq, _ = q.shape
    nt = seq // tile
    masks = _gla_masks(reverse)
    blk = (lambda b, t: (b, nt - 1 - t, 0)) if reverse else (lambda b, t: (b, t, 0))
    bcol = 1 if reverse else 0
    bblk = (lambda b, t: (b, nt - 1 - t, bcol)) if reverse else (lambda b, t: (b, t, bcol))
    kernel = functools.partial(_gla_kernel, reverse=reverse, tile=tile)
    return pl.pallas_call(
        kernel,
        grid=(bsz, nt),
        in_specs=[
            pl.BlockSpec((1, tile, GLA_DK), blk),
            pl.BlockSpec((1, tile, GLA_DK), blk),
            pl.BlockSpec((1, tile, GLA_DV), blk),
            pl.BlockSpec((1, tile, GLA_DK), bblk),
            _const_spec(masks.shape),
        ],
        out_specs=pl.BlockSpec((1, tile, GLA_DV), blk),
        out_shape=jax.ShapeDtypeStruct((bsz, seq, GLA_DV), BF16),
        scratch_shapes=[
            pltpu.VMEM((GLA_HEADS, GLA_HK, GLA_HV), F32),
            pltpu.VMEM((GLA_HEADS, GLA_CHUNK, GLA_HK), F32),
            pltpu.VMEM((GLA_HEADS, GLA_CHUNK, GLA_HK), F32),
        ],
        compiler_params=pltpu.CompilerParams(
            dimension_semantics=("parallel", "arbitrary"), vmem_limit_bytes=VMEM_LIMIT_BYTES),
        name="gla_bwd" if reverse else "gla_fwd",
    )(q, k, v, bcum, masks)


def _attn_kernel(q_ref, k_ref, vt_ref, lq_ref, lk_ref, gain_ref, o_ref,
                 q2_ref, s_ref, smax_ref, m_ref, acc_ref, *, tq, tk, seq, lam_init):
    q = q_ref[0]
    lane = lax.broadcasted_iota(jnp.int32, (tq, LANES), 1)
    zero = jnp.zeros_like(q)
    q2_ref[0:tq] = jnp.where(lane < DIFF_HEAD_DIM, q, zero)
    q2_ref[tq:2 * tq] = jnp.where(lane >= DIFF_HEAD_DIM, q, zero)
    m_ref[...] = jnp.full_like(m_ref, -jnp.inf)
    acc_ref[...] = jnp.zeros_like(acc_ref)
    n_tiles = seq // tk

    def scores(kt, slot):
        ks = pl.multiple_of(kt * tk, tk)
        k = k_ref[0, pl.ds(ks, tk), :]
        s = lax.dot_general(k, q2_ref[...], NT_DIMS, preferred_element_type=F32)
        s_ref[slot] = s
        smax_ref[slot] = jnp.max(s, axis=0, keepdims=True)

    ones_rows = jnp.ones((ONES_ROWS, tk), BF16)

    def softmax_values(kt, slot):
        ks = pl.multiple_of(kt * tk, tk)
        vt = jnp.concatenate([vt_ref[0, :, pl.ds(ks, tk)], ones_rows], axis=0)
        m_prev = m_ref[...]
        m_new = jnp.maximum(m_prev, smax_ref[slot])
        alpha = jnp.exp2(m_prev - m_new)
        p = jnp.exp2(s_ref[slot] - m_new).astype(BF16)
        m_ref[...] = m_new
        pv = None
        kc = min(MXU_DEPTH, tk)
        for c in range(tk // kc):
            rows = slice(c * kc, (c + 1) * kc)
            d = jnp.dot(vt[:, rows], p[rows, :], preferred_element_type=F32)
            pv = d if pv is None else pv + d
        acc_ref[...] = alpha * acc_ref[...] + pv

    def step(i, slot):
        scores(i + 1, 1 - slot)
        softmax_values(i, slot)

    scores(0, 0)

    def pair(j, carry):
        i = 2 * j
        step(i, 0)
        step(i + 1, 1)
        return carry

    lax.fori_loop(0, (n_tiles - 2) // 2, pair, 0)
    step(n_tiles - 2, 0)
    softmax_values(n_tiles - 1, 1)

    o2 = acc_ref[0:DIFF_HV] / acc_ref[DIFF_HV:DIFF_HV + 1]
    dots = jnp.sum(lq_ref[...] * lk_ref[...], axis=1, keepdims=True)
    e = jnp.exp(dots)
    lam = e[0:1, :] - e[1:2, :] + lam_init
    ot = o2[:, 0:tq] - lam * o2[:, tq:2 * tq]
    ms = jnp.mean(ot * ot, axis=0, keepdims=True)
    on = jnp.transpose(ot * lax.rsqrt(ms + EPS))
    o_ref[0] = (on * gain_ref[...] * (1.0 - lam_init)).astype(o_ref.dtype)


def _attn(dq, dk, dvt, lq, lk, gain, *, lam_init, tq, tk):
    bsz, seq, _ = dq.shape
    assert seq % (2 * tk) == 0
    kernel = functools.partial(_attn_kernel, tq=tq, tk=tk, seq=seq, lam_init=lam_init)
    return pl.pallas_call(
        kernel,
        grid=(bsz, DIFF_HEADS, seq // tq),
        in_specs=[
            pl.BlockSpec((1, tq, DIFF_HV), lambda b, h, i: (b, i, h)),
            pl.BlockSpec((1, seq, DIFF_HV), lambda b, h, i: (b, 0, h)),
            pl.BlockSpec((1, DIFF_HV, seq), lambda b, h, i: (b, h, 0)),
            pl.BlockSpec((2, DIFF_HEAD_DIM), lambda b, h, i: (0, 0)),
            pl.BlockSpec((2, DIFF_HEAD_DIM), lambda b, h, i: (0, 0)),
            pl.BlockSpec((1, DIFF_HV), lambda b, h, i: (0, 0)),
        ],
        out_specs=pl.BlockSpec((1, tq, DIFF_HV), lambda b, h, i: (b, i, h)),
        out_shape=jax.ShapeDtypeStruct((bsz, seq, D_MODEL), BF16),
        scratch_shapes=[
            pltpu.VMEM((2 * tq, DIFF_HV), BF16),
            pltpu.VMEM((2, tk, 2 * tq), F32),
            pltpu.VMEM((2, 1, 2 * tq), F32),
            pltpu.VMEM((1, 2 * tq), F32),
            pltpu.VMEM((DIFF_HV + ONES_ROWS, 2 * tq), F32),
        ],
        compiler_params=pltpu.CompilerParams(
            dimension_semantics=("parallel", "parallel", "arbitrary"),
            vmem_limit_bytes=VMEM_LIMIT_BYTES),
        name="diff_attn",
    )(dq, dk, dvt, lq, lk, gain)


def _out_kernel(x_ref, gate_ref, of_ref, ob_ref, z_ref, od_ref, dz_ref, mg_ref, md_ref,
                ggain_ref, fgain_ref, wbg_ref, wbd_ref, wo_ref, y_ref):
    og = of_ref[0].astype(F32) + ob_ref[0].astype(F32)
    ggain = ggain_ref[...]
    heads = []
    for h in range(GLA_HEADS):
        oh = og[:, h * GLA_HV:(h + 1) * GLA_HV]
        ms = jnp.mean(oh * oh, axis=1, keepdims=True)
        heads.append(oh * lax.rsqrt(ms + EPS) * ggain)
    og = jnp.concatenate(heads, axis=1) * z_ref[0].astype(F32)
    y_gla = jnp.dot(og.astype(BF16), wbg_ref[...], preferred_element_type=F32)
    od = od_ref[0].astype(F32) * dz_ref[0].astype(F32)
    y_diff = jnp.dot(od.astype(BF16), wbd_ref[...], preferred_element_type=F32)
    merged = mg_ref[0].astype(F32) * y_gla + md_ref[0].astype(F32) * y_diff
    y = jnp.dot(merged.astype(BF16), wo_ref[...], preferred_element_type=F32)
    r = x_ref[0] + gate_ref[0] * y
    ms = jnp.mean(r * r, axis=1, keepdims=True)
    y_ref[0] = r * lax.rsqrt(ms + EPS) * fgain_ref[...]


def _out_stage(x, gate, o_f, o_b, z, o_d, dz, mg, md, ggain, fgain, wbg, wbd, wo, *, tile):
    bsz, seq, _ = x.shape
    row = pl.BlockSpec((1, tile, D_MODEL), lambda b, i: (b, i, 0))
    per_b = pl.BlockSpec((1, 1, D_MODEL), lambda b, i: (b, 0, 0))
    return pl.pallas_call(
        _out_kernel,
        grid=(bsz, seq // tile),
        in_specs=[row, per_b] + [row] * 7 + [
            _const_spec(ggain.shape), _const_spec(fgain.shape),
            _const_spec(wbg.shape), _const_spec(wbd.shape), _const_spec(wo.shape)],
        out_specs=row,
        out_shape=jax.ShapeDtypeStruct((bsz, seq, D_MODEL), F32),
        compiler_params=pltpu.CompilerParams(
            dimension_semantics=("parallel", "parallel"), vmem_limit_bytes=VMEM_LIMIT_BYTES),
        name="out_stage",
    )(x, gate, o_f, o_b, z, o_d, dz, mg, md, ggain, fgain, wbg, wbd, wo)


def _rope_tables(seq):
    half = DIFF_HEAD_DIM // 2
    inv = 1.0 / (ROPE_THETA ** (jnp.arange(half, dtype=F32) / half))
    ang = jnp.arange(seq, dtype=F32)[:, None] * inv[None, :]
    cos = jnp.cos(ang)
    sin = jnp.sin(ang)
    cs = jnp.concatenate([cos, cos, cos, cos], axis=1)
    sn = jnp.concatenate([-sin, sin, -sin, sin], axis=1)
    return cs, sn


def _layer_weights(w_in, w_alpha, b_alpha):
    splits = (GLA_DK, GLA_DK, GLA_DV, GLA_DV, 2 * GLA_RANK) + (D_MODEL,) * 6
    pts = np.cumsum(splits)[:-1].tolist()
    a_q, a_k, a_v, a_z, a_low, d_q, d_k, d_v, d_z, m_g, m_d = jnp.split(w_in, pts, axis=1)
    wg = jnp.concatenate([a_q, a_k, a_v, a_z], axis=1).astype(BF16)
    zeros = jnp.zeros((GLA_RANK, GLA_DK), w_alpha.dtype)
    walpha = jnp.concatenate([
        jnp.concatenate([w_alpha[0], zeros], axis=1),
        jnp.concatenate([zeros, w_alpha[1]], axis=1)], axis=0).astype(BF16)
    balpha = b_alpha.reshape(1, 2 * GLA_DK)
    return (wg, a_low.astype(BF16), walpha, balpha,
            d_q.astype(BF16), d_k.astype(BF16), d_v.T.astype(BF16), d_z.astype(BF16),
            m_g.astype(BF16), m_d.astype(BF16))


def _trunk_layer(x, mod, lam_init, norm_gain, in_weights, gla_norm_gain, lambda_q, lambda_k,
                 diff_norm_gain, wbg, wbd, wo, final_gain, *, row_tile, gla_tile, tq, tk):
    bsz, seq, _ = x.shape
    row_tile, gla_tile, tq, tk = (min(t, s) for t, s in
                                  ((row_tile, seq), (gla_tile, seq), (tq, seq), (tk, seq // 2)))
    shift, scale, gate = [m.reshape(bsz, 1, D_MODEL) for m in jnp.split(mod, 3, axis=-1)]
    cs, sn = _rope_tables(seq)
    (q, k, v, z, bcum, dq, dk, dv, dz, mg, md) = _inproj(
        x, shift, scale, norm_gain.reshape(1, D_MODEL), cs, sn, in_weights, row_tile)
    o_f = _gla(q, k, v, bcum, reverse=False, tile=gla_tile)
    o_b = _gla(q, k, v, bcum, reverse=True, tile=gla_tile)
    o_d = _attn(dq, dk, dv, lambda_q, lambda_k, diff_norm_gain.reshape(1, DIFF_HV),
                lam_init=lam_init, tq=tq, tk=tk)
    return _out_stage(x, gate, o_f, o_b, z, o_d, dz, mg, md,
                      gla_norm_gain.reshape(1, GLA_HV), final_gain.reshape(1, D_MODEL),
                      wbg, wbd, wo, tile=row_tile)


def kernel(x_prompt, x_sample, c_prompt, c_sample, w_ada, b_ada, norm_gain, w_in, w_alpha, b_alpha,
           gla_norm_gain, lambda_q, lambda_k, diff_norm_gain, w_bo_gla, w_bo_diff, w_out, final_gain):
    depth = w_in.shape[0]
    assert depth == 1, "final RMSNorm is fused into the (single) layer's output stage"
    layer = 0
    lam_init = 0.8 - 0.6 * math.exp(-0.3 * layer)
    nb_p, nb_s = c_prompt.shape[0], c_sample.shape[0]
    rows = -(-(nb_p + nb_s) // 8) * 8
    c_all = jnp.concatenate(
        [c_prompt, c_sample, jnp.zeros((rows - nb_p - nb_s, D_MODEL), F32)], axis=0)
    mod = _ada(c_all, w_ada[layer].astype(BF16), b_ada[layer].reshape(1, 3 * D_MODEL))
    in_weights = _layer_weights(w_in[layer], w_alpha[layer], b_alpha[layer])
    common = dict(
        lam_init=lam_init, norm_gain=norm_gain[layer], in_weights=in_weights,
        gla_norm_gain=gla_norm_gain[layer], lambda_q=lambda_q[layer], lambda_k=lambda_k[layer],
        diff_norm_gain=diff_norm_gain[layer], wbg=w_bo_gla[layer].astype(BF16),
        wbd=w_bo_diff[layer].astype(BF16), wo=w_out[layer].astype(BF16), final_gain=final_gain)
    y_prompt = _trunk_layer(x_prompt, mod[:nb_p], row_tile=256, gla_tile=512, tq=512, tk=1024, **common)
    y_sample = _trunk_layer(x_sample, mod[nb_p:nb_p + nb_s], row_tile=256, gla_tile=512, tq=1024, tk=512,
                            **common)
    return (y_prompt, y_sample)
```

```python
import functools
import math

import numpy as np
import jax
import jax.numpy as jnp
from jax import lax
from jax.experimental import pallas as pl
from jax.experimental.pallas import tpu as pltpu

D_MODEL = 1024
GLA_HEADS = 4
GLA_DK = D_MODEL // 2
GLA_DV = D_MODEL
GLA_HK = GLA_DK // GLA_HEADS
GLA_HV = GLA_DV // GLA_HEADS
GLA_RANK = 16
GLA_LOGIT_NORM = 16.0
GLA_CHUNK = 64
GLA_SUB = 8
GLA_LEVELS = (1, 2, 4, 8, 16, 32)
DIFF_HEAD_DIM = 64
DIFF_HEADS = D_MODEL // (2 * DIFF_HEAD_DIM)
DIFF_HV = 2 * DIFF_HEAD_DIM
ROPE_THETA = 10000.0
EPS = 1e-6
LANES = 128
ONES_ROWS = 16
MXU_DEPTH = 256
ATTN_Q_SCALE = (DIFF_HEAD_DIM ** -0.5) * math.log2(math.e)

F32 = jnp.float32
BF16 = jnp.bfloat16

VMEM_LIMIT_BYTES = 56 * 1024 * 1024

NT_DIMS = (((1,), (1,)), ((), ()))
TN_DIMS = (((0,), (0,)), ((), ()))


def _sigmoid(x):
    return 1.0 / (1.0 + jnp.exp(-x))


def _const_spec(shape):
    nd = len(shape)
    return pl.BlockSpec(shape, lambda *_: (0,) * nd, pipeline_mode=pl.Buffered(1))


def _ada_kernel(c_ref, w_ref, b_ref, o_ref):
    c = c_ref[...]
    s = c * _sigmoid(c)
    o_ref[...] = jnp.dot(s.astype(BF16), w_ref[...], preferred_element_type=F32) + b_ref[...]


def _ada(c, w_bf16, b):
    rows = c.shape[0]
    n_out = w_bf16.shape[1]
    tn = D_MODEL
    return pl.pallas_call(
        _ada_kernel,
        grid=(n_out // tn,),
        in_specs=[
            pl.BlockSpec((rows, D_MODEL), lambda j: (0, 0)),
            pl.BlockSpec((D_MODEL, tn), lambda j: (0, j)),
            pl.BlockSpec((1, tn), lambda j: (0, j)),
        ],
        out_specs=pl.BlockSpec((rows, tn), lambda j: (0, j)),
        out_shape=jax.ShapeDtypeStruct((rows, n_out), F32),
        name="ada",
    )(c, w_bf16, b)


def _rope_slices(x, cs, sn):
    lane = lax.broadcasted_iota(jnp.int32, (x.shape[0], LANES), 1)
    is_x2 = (lane & 32) != 0
    outs = []
    for s in range(x.shape[1] // LANES):
        xs = x[:, s * LANES:(s + 1) * LANES]
        from_lo = pltpu.roll(xs, 32, axis=1)
        from_hi = pltpu.roll(xs, LANES - 32, axis=1)
        partner = jnp.where(is_x2, from_lo, from_hi)
        outs.append(xs * cs + partner * sn)
    return jnp.concatenate(outs, axis=1)


def _inproj_kernel(x_ref, shift_ref, scale_ref, gain_ref, cs_ref, sn_ref, trif_ref, trib_ref,
                   wg_ref, wlow_ref, walpha_ref, balpha_ref,
                   wdq_ref, wdk_ref, wdv_ref, wdz_ref, wmg_ref, wmd_ref,
                   q_out, k_out, v_out, z_out, b_out,
                   dq_out, dk_out, dv_out, dz_out, mg_out, md_out):
    x = x_ref[0]
    ms = jnp.mean(x * x, axis=-1, keepdims=True)
    y = x * lax.rsqrt(ms + EPS) * gain_ref[...]
    h = (y * (1.0 + scale_ref[0]) + shift_ref[0]).astype(BF16)

    pg = jnp.dot(h, wg_ref[...], preferred_element_type=F32)
    q_out[0] = (pg[:, 0:GLA_DK] * (GLA_HK ** -0.5)).astype(BF16)
    k_out[0] = pg[:, GLA_DK:2 * GLA_DK].astype(BF16)
    v_out[0] = pg[:, 2 * GLA_DK:2 * GLA_DK + GLA_DV].astype(BF16)
    az = pg[:, 2 * GLA_DK + GLA_DV:]
    z_out[0] = (az * _sigmoid(az)).astype(BF16)

    low = jnp.dot(h, wlow_ref[...], preferred_element_type=F32)
    logits = jnp.dot(low.astype(BF16), walpha_ref[...], preferred_element_type=F32) + balpha_ref[...]
    g = (jnp.minimum(logits, 0.0) - jnp.log1p(jnp.exp(-jnp.abs(logits)))) * (1.0 / GLA_LOGIT_NORM)
    g_hi = g.astype(BF16)
    g_lo = (g - g_hi.astype(F32)).astype(BF16)
    for tri_ref, lo in ((trif_ref, 0), (trib_ref, GLA_DK)):
        tri = tri_ref[...]
        b_out[0, :, lo:lo + GLA_DK] = (
            jnp.dot(tri, g_hi[:, lo:lo + GLA_DK], preferred_element_type=F32)
            + jnp.dot(tri, g_lo[:, lo:lo + GLA_DK], preferred_element_type=F32))

    cs = cs_ref[...]
    sn = sn_ref[...]
    dq = jnp.dot(h, wdq_ref[...], preferred_element_type=F32)
    dq_out[0] = (_rope_slices(dq, cs, sn) * ATTN_Q_SCALE).astype(BF16)
    dk = jnp.dot(h, wdk_ref[...], preferred_element_type=F32)
    dk_out[0] = _rope_slices(dk, cs, sn).astype(BF16)
    dv_out[0] = lax.dot_general(wdv_ref[...], h, NT_DIMS, preferred_element_type=F32).astype(BF16)
    dz = jnp.dot(h, wdz_ref[...], preferred_element_type=F32)
    dz_out[0] = (dz * _sigmoid(dz)).astype(BF16)
    mg_out[0] = _sigmoid(jnp.dot(h, wmg_ref[...], preferred_element_type=F32)).astype(BF16)
    md_out[0] = _sigmoid(jnp.dot(h, wmd_ref[...], preferred_element_type=F32)).astype(BF16)


def _chunk_tri(tile, reverse):
    i = np.arange(tile)[:, None]
    j = np.arange(tile)[None, :]
    same = (i // GLA_CHUNK) == (j // GLA_CHUNK)
    tri = (j >= i) if reverse else (j <= i)
    return jnp.asarray(same & tri, dtype=BF16)


def _inproj(x, shift, scale, gain, cs, sn, weights, tile):
    bsz, seq, _ = x.shape
    (wg, wlow, walpha, balpha, wdq, wdk, wdv, wdz, wmg, wmd) = weights
    trif = _chunk_tri(tile, False)
    trib = _chunk_tri(tile, True)
    row = lambda c: pl.BlockSpec((1, tile, c), lambda b, i: (b, i, 0))
    per_b = pl.BlockSpec((1, 1, D_MODEL), lambda b, i: (b, 0, 0))
    consts = [gain, None, None, trif, trib, wg, wlow, walpha, balpha, wdq, wdk, wdv, wdz, wmg, wmd]
    in_specs = [row(D_MODEL), per_b, per_b]
    for a in consts:
        if a is None:
            in_specs.append(pl.BlockSpec((tile, LANES), lambda b, i: (i, 0)))
        else:
            in_specs.append(_const_spec(a.shape))
    out_cols = [(GLA_DK, BF16), (GLA_DK, BF16), (GLA_DV, BF16), (GLA_DV, BF16), (2 * GLA_DK, F32)] + \
               [(D_MODEL, BF16)] * 6
    out_specs = [row(c) for c, _ in out_cols]
    out_shape = [jax.ShapeDtypeStruct((bsz, seq, c), dt) for c, dt in out_cols]
    dv_index = 7
    out_specs[dv_index] = pl.BlockSpec((1, D_MODEL, tile), lambda b, i: (b, 0, i))
    out_shape[dv_index] = jax.ShapeDtypeStruct((bsz, D_MODEL, seq), BF16)
    return pl.pallas_call(
        _inproj_kernel,
        grid=(bsz, seq // tile),
        in_specs=in_specs,
        out_specs=out_specs,
        out_shape=out_shape,
        compiler_params=pltpu.CompilerParams(
            dimension_semantics=("parallel", "parallel"), vmem_limit_bytes=VMEM_LIMIT_BYTES),
        name="inproj",
    )(x, shift, scale, gain, cs, sn, trif, trib, wg, wlow, walpha, balpha, wdq, wdk, wdv, wdz, wmg, wmd)


def _gla_masks(reverse):
    c = GLA_CHUNK
    i = np.arange(c)[:, None]
    j = np.arange(c)[None, :]
    q_half, k_half = (0, 1) if reverse else (1, 0)
    masks = []
    for s in GLA_LEVELS:
        same_parent = (i // (2 * s)) == (j // (2 * s))
        masks.append(same_parent & ((i // s) % 2 == q_half) & ((j // s) % 2 == k_half))
    masks.append(i == j)
    return jnp.asarray(np.stack(masks).astype(np.float32))


def _gla_kernel(qf_ref, kf_ref, vf_ref, bf_ref, qb_ref, kb_ref, vb_ref, bb_ref, mask_ref,
                of_ref, ob_ref, state_ref, b32_ref, *, tile):
    c = GLA_CHUNK
    n_chunks = tile // c
    refs = {False: (qf_ref, kf_ref, vf_ref, bf_ref, of_ref), True: (qb_ref, kb_ref, vb_ref, bb_ref, ob_ref)}
    chains = [(reverse, h) for reverse in (False, True) for h in range(GLA_HEADS)]

    @pl.when(pl.program_id(1) == 0)
    def _():
        state_ref[...] = jnp.zeros_like(state_ref)

    sub8 = lax.broadcasted_iota(jnp.int32, (GLA_SUB, GLA_HK), 0)

    def chunk_body(ci, carry):
        row0 = {False: pl.multiple_of(ci * c, c), True: pl.multiple_of((n_chunks - 1 - ci) * c, c)}
        staged = []
        for reverse, h in chains:
            q_ref, k_ref, v_ref, b_ref, o_ref = refs[reverse]
            d = int(reverse)
            rows = pl.ds(row0[reverse], c)
            scan_last = 0 if reverse else c - 1
            lk = slice(h * GLA_HK, (h + 1) * GLA_HK)
            lv = slice(h * GLA_HV, (h + 1) * GLA_HV)
            q = q_ref[0, rows, lk].astype(F32)
            k = k_ref[0, rows, lk].astype(F32)
            v = v_ref[0, rows, lv]
            b = b_ref[0, rows, lk]
            b32_ref[d, h] = b

            def b_row(r, d=d, h=h):
                return b32_ref[d, h, pl.ds(r, 1), :]

            b_tot = b_row(scan_last)

            state = state_ref[d, h]
            q_in = (q * jnp.exp(b)).astype(BF16)
            o_inter = jnp.dot(q_in, state.astype(BF16), preferred_element_type=F32)

            diag = jnp.sum(q * k, axis=1, keepdims=True)
            level_scores = []
            for s in GLA_LEVELS:
                off = s if reverse else s - 1
                blocks = []
                for jb in range(c // GLA_SUB):
                    r0 = jb * GLA_SUB
                    if 2 * s >= GLA_SUB:
                        blk = jnp.broadcast_to(b_row(r0 // (2 * s) * (2 * s) + off), (GLA_SUB, GLA_HK))
                    else:
                        blk = jnp.broadcast_to(b_row(r0 + off), (GLA_SUB, GLA_HK))
                        for t in range(1, GLA_SUB // (2 * s)):
                            blk = jnp.where(sub8 >= 2 * s * t, b_row(r0 + 2 * s * t + off), blk)
                    blocks.append(blk)
                ref = jnp.concatenate(blocks, axis=0)
                e = jnp.exp(-jnp.abs(b - ref))
                level_scores.append(lax.dot_general(
                    (q * e).astype(BF16), (k * e).astype(BF16), NT_DIMS, preferred_element_type=F32))

            k_hat = (k * jnp.exp(b_tot - b)).astype(BF16)
            delta = lax.dot_general(k_hat, v, TN_DIMS, preferred_element_type=F32)
            decay_col = jnp.transpose(jnp.broadcast_to(jnp.exp(b_tot), (GLA_HK, GLA_HK)))
            decay = jnp.concatenate([decay_col] * (GLA_HV // GLA_HK), axis=1)
            staged.append((d, h, o_ref, rows, lv, v, o_inter, diag, level_scores, state * decay + delta))

        for d, h, o_ref, rows, lv, v, o_inter, diag, level_scores, new_state in staged:
            scores = mask_ref[d, len(GLA_LEVELS)] * diag
            for lvl, ls in enumerate(level_scores):
                scores = scores + mask_ref[d, lvl] * ls
            o = o_inter + jnp.dot(scores.astype(BF16), v, preferred_element_type=F32)
            o_ref[0, rows, lv] = o.astype(o_ref.dtype)
            state_ref[d, h] = new_state
        return carry

    lax.fori_loop(0, n_chunks, chunk_body, 0)


def _gla(q, k, v, bcum, *, tile):
    bsz, seq, _ = q.shape
    nt = seq // tile
    masks = jnp.stack([_gla_masks(False), _gla_masks(True)])
    fwd = lambda b, t: (b, t, 0)
    bwd = lambda b, t: (b, nt - 1 - t, 0)
    bwd_b = lambda b, t: (b, nt - 1 - t, 1)
    kernel = functools.partial(_gla_kernel, tile=tile)
    out = jax.ShapeDtypeStruct((bsz, seq, GLA_DV), BF16)
    return pl.pallas_call(
        kernel,
        grid=(bsz, nt),
        in_specs=[
            pl.BlockSpec((1, tile, GLA_DK), fwd), pl.BlockSpec((1, tile, GLA_DK), fwd),
            pl.BlockSpec((1, tile, GLA_DV), fwd), pl.BlockSpec((1, tile, GLA_DK), fwd),
            pl.BlockSpec((1, tile, GLA_DK), bwd), pl.BlockSpec((1, tile, GLA_DK), bwd),
            pl.BlockSpec((1, tile, GLA_DV), bwd), pl.BlockSpec((1, tile, GLA_DK), bwd_b),
            _const_spec(masks.shape),
        ],
        out_specs=[pl.BlockSpec((1, tile, GLA_DV), fwd), pl.BlockSpec((1, tile, GLA_DV), bwd)],
        out_shape=[out, out],
        scratch_shapes=[
            pltpu.VMEM((2, GLA_HEADS, GLA_HK, GLA_HV), F32),
            pltpu.VMEM((2, GLA_HEADS, GLA_CHUNK, GLA_HK), F32),
        ],
        compiler_params=pltpu.CompilerParams(
            dimension_semantics=("parallel", "arbitrary"), vmem_limit_bytes=VMEM_LIMIT_BYTES),
        name="gla",
    )(q, k, v, bcum, q, k, v, bcum, masks)


def _attn_kernel(q_ref, k_ref, vt_ref, lq_ref, lk_ref, gain_ref, o_ref,
                 q2_ref, s_ref, smax_ref, m_ref, acc_ref, *, tq, tk, seq, lam_init):
    q = q_ref[0]
    lane = lax.broadcasted_iota(jnp.int32, (tq, LANES), 1)
    zero = jnp.zeros_like(q)
    q2_ref[0:tq] = jnp.where(lane < DIFF_HEAD_DIM, q, zero)
    q2_ref[tq:2 * tq] = jnp.where(lane >= DIFF_HEAD_DIM, q, zero)
    m_ref[...] = jnp.full_like(m_ref, -jnp.inf)
    acc_ref[...] = jnp.zeros_like(acc_ref)
    n_tiles = seq // tk

    kc = min(MXU_DEPTH, tk)
    n_kc = tk // kc
    ones_rows = jnp.ones((ONES_ROWS, kc), BF16)

    def scores_chunk(kt, slot, c, smax):
        ks = pl.multiple_of(kt * tk, tk)
        k = k_ref[0, pl.ds(ks + c * kc, kc), :]
        s = lax.dot_general(k, q2_ref[...], NT_DIMS, preferred_element_type=F32)
        s_ref[slot, c * kc:(c + 1) * kc, :] = s
        cmax = jnp.max(s, axis=0, keepdims=True)
        return cmax if smax is None else jnp.maximum(smax, cmax)

    def values_chunk(kt, slot, c, m_new, pv):
        ks = pl.multiple_of(kt * tk, tk)
        vt = jnp.concatenate([vt_ref[0, :, pl.ds(ks + c * kc, kc)], ones_rows], axis=0)
        p = jnp.exp2(s_ref[slot, c * kc:(c + 1) * kc, :] - m_new).astype(BF16)
        d = jnp.dot(vt, p, preferred_element_type=F32)
        return d if pv is None else pv + d

    def step(i, slot, with_scores=True):
        m_prev = m_ref[...]
        m_new = jnp.maximum(m_prev, smax_ref[slot])
        alpha = jnp.exp2(m_prev - m_new)
        m_ref[...] = m_new
        smax, pv = None, None
        for c in range(n_kc):
            if with_scores:
                smax = scores_chunk(i + 1, 1 - slot, c, smax)
            pv = values_chunk(i, slot, c, m_new, pv)
        if with_scores:
            smax_ref[1 - slot] = smax
        acc_ref[...] = alpha * acc_ref[...] + pv

    smax0 = None
    for c in range(n_kc):
        smax0 = scores_chunk(0, 0, c, smax0)
    smax_ref[0] = smax0

    def pair(j, carry):
        i = 2 * j
        step(i, 0)
        step(i + 1, 1)
        return carry

    lax.fori_loop(0, (n_tiles - 2) // 2, pair, 0)
    step(n_tiles - 2, 0)
    step(n_tiles - 1, 1, with_scores=False)

    o2 = acc_ref[0:DIFF_HV] / acc_ref[DIFF_HV:DIFF_HV + 1]
    dots = jnp.sum(lq_ref[...] * lk_ref[...], axis=1, keepdims=True)
    e = jnp.exp(dots)
    lam = e[0:1, :] - e[1:2, :] + lam_init
    ot = o2[:, 0:tq] - lam * o2[:, tq:2 * tq]
    ms = jnp.mean(ot * ot, axis=0, keepdims=True)
    on = jnp.transpose(ot * lax.rsqrt(ms + EPS))
    o_ref[0] = (on * gain_ref[...] * (1.0 - lam_init)).astype(o_ref.dtype)


def _attn(dq, dk, dvt, lq, lk, gain, *, lam_init, tq, tk):
    bsz, seq, _ = dq.shape
    assert seq % (2 * tk) == 0
    kernel = functools.partial(_attn_kernel, tq=tq, tk=tk, seq=seq, lam_init=lam_init)
    return pl.pallas_call(
        kernel,
        grid=(bsz, DIFF_HEADS, seq // tq),
        in_specs=[
            pl.BlockSpec((1, tq, DIFF_HV), lambda b, h, i: (b, i, h)),
            pl.BlockSpec((1, seq, DIFF_HV), lambda b, h, i: (b, 0, h)),
            pl.BlockSpec((1, DIFF_HV, seq), lambda b, h, i: (b, h, 0)),
            pl.BlockSpec((2, DIFF_HEAD_DIM), lambda b, h, i: (0, 0)),
            pl.BlockSpec((2, DIFF_HEAD_DIM), lambda b, h, i: (0, 0)),
            pl.BlockSpec((1, DIFF_HV), lambda b, h, i: (0, 0)),
        ],
        out_specs=pl.BlockSpec((1, tq, DIFF_HV), lambda b, h, i: (b, i, h)),
        out_shape=jax.ShapeDtypeStruct((bsz, seq, D_MODEL), BF16),
        scratch_shapes=[
            pltpu.VMEM((2 * tq, DIFF_HV), BF16),
            pltpu.VMEM((2, tk, 2 * tq), F32),
            pltpu.VMEM((2, 1, 2 * tq), F32),
            pltpu.VMEM((1, 2 * tq), F32),
            pltpu.VMEM((DIFF_HV + ONES_ROWS, 2 * tq), F32),
        ],
        compiler_params=pltpu.CompilerParams(
            dimension_semantics=("parallel", "parallel", "arbitrary"),
            vmem_limit_bytes=VMEM_LIMIT_BYTES),
        name="diff_attn",
    )(dq, dk, dvt, lq, lk, gain)


def _out_kernel(x_ref, gate_ref, of_ref, ob_ref, z_ref, od_ref, dz_ref, mg_ref, md_ref,
                ggain_ref, fgain_ref, wbg_ref, wbd_ref, wo_ref, y_ref):
    og = of_ref[0].astype(F32) + ob_ref[0].astype(F32)
    ggain = ggain_ref[...]
    heads = []
    for h in range(GLA_HEADS):
        oh = og[:, h * GLA_HV:(h + 1) * GLA_HV]
        ms = jnp.mean(oh * oh, axis=1, keepdims=True)
        heads.append(oh * lax.rsqrt(ms + EPS) * ggain)
    og = jnp.concatenate(heads, axis=1) * z_ref[0].astype(F32)
    y_gla = jnp.dot(og.astype(BF16), wbg_ref[...], preferred_element_type=F32)
    od = od_ref[0].astype(F32) * dz_ref[0].astype(F32)
    y_diff = jnp.dot(od.astype(BF16), wbd_ref[...], preferred_element_type=F32)
    merged = mg_ref[0].astype(F32) * y_gla + md_ref[0].astype(F32) * y_diff
    y = jnp.dot(merged.astype(BF16), wo_ref[...], preferred_element_type=F32)
    r = x_ref[0] + gate_ref[0] * y
    ms = jnp.mean(r * r, axis=1, keepdims=True)
    y_ref[0] = r * lax.rsqrt(ms + EPS) * fgain_ref[...]


def _out_stage(x, gate, o_f, o_b, z, o_d, dz, mg, md, ggain, fgain, wbg, wbd, wo, *, tile):
    bsz, seq, _ = x.shape
    row = pl.BlockSpec((1, tile, D_MODEL), lambda b, i: (b, i, 0))
    per_b = pl.BlockSpec((1, 1, D_MODEL), lambda b, i: (b, 0, 0))
    return pl.pallas_call(
        _out_kernel,
        grid=(bsz, seq // tile),
        in_specs=[row, per_b] + [row] * 7 + [
            _const_spec(ggain.shape), _const_spec(fgain.shape),
            _const_spec(wbg.shape), _const_spec(wbd.shape), _const_spec(wo.shape)],
        out_specs=row,
        out_shape=jax.ShapeDtypeStruct((bsz, seq, D_MODEL), F32),
        compiler_params=pltpu.CompilerParams(
            dimension_semantics=("parallel", "parallel"), vmem_limit_bytes=VMEM_LIMIT_BYTES),
        name="out_stage",
    )(x, gate, o_f, o_b, z, o_d, dz, mg, md, ggain, fgain, wbg, wbd, wo)


def _rope_tables(seq):
    half = DIFF_HEAD_DIM // 2
    inv = 1.0 / (ROPE_THETA ** (jnp.arange(half, dtype=F32) / half))
    ang = jnp.arange(seq, dtype=F32)[:, None] * inv[None, :]
    cos = jnp.cos(ang)
    sin = jnp.sin(ang)
    cs = jnp.concatenate([cos, cos, cos, cos], axis=1)
    sn = jnp.concatenate([-sin, sin, -sin, sin], axis=1)
    return cs, sn


def _layer_weights(w_in, w_alpha, b_alpha):
    splits = (GLA_DK, GLA_DK, GLA_DV, GLA_DV, 2 * GLA_RANK) + (D_MODEL,) * 6
    pts = np.cumsum(splits)[:-1].tolist()
    a_q, a_k, a_v, a_z, a_low, d_q, d_k, d_v, d_z, m_g, m_d = jnp.split(w_in, pts, axis=1)
    wg = jnp.concatenate([a_q, a_k, a_v, a_z], axis=1).astype(BF16)
    zeros = jnp.zeros((GLA_RANK, GLA_DK), w_alpha.dtype)
    walpha = jnp.concatenate([
        jnp.concatenate([w_alpha[0], zeros], axis=1),
        jnp.concatenate([zeros, w_alpha[1]], axis=1)], axis=0).astype(BF16)
    balpha = b_alpha.reshape(1, 2 * GLA_DK)
    return (wg, a_low.astype(BF16), walpha, balpha,
            d_q.astype(BF16), d_k.astype(BF16), d_v.T.astype(BF16), d_z.astype(BF16),
            m_g.astype(BF16), m_d.astype(BF16))


def _trunk_layer(x, mod, lam_init, norm_gain, in_weights, gla_norm_gain, lambda_q, lambda_k,
                 diff_norm_gain, wbg, wbd, wo, final_gain, *, row_tile, gla_tile, tq, tk):
    bsz, seq, _ = x.shape
    row_tile, gla_tile, tq, tk = (min(t, s) for t, s in
                                  ((row_tile, seq), (gla_tile, seq), (tq, seq), (tk, seq // 2)))
    shift, scale, gate = [m.reshape(bsz, 1, D_MODEL) for m in jnp.split(mod, 3, axis=-1)]
    cs, sn = _rope_tables(seq)
    (q, k, v, z, bcum, dq, dk, dv, dz, mg, md) = _inproj(
        x, shift, scale, norm_gain.reshape(1, D_MODEL), cs, sn, in_weights, row_tile)
    o_f, o_b = _gla(q, k, v, bcum, tile=gla_tile)
    o_d = _attn(dq, dk, dv, lambda_q, lambda_k, diff_norm_gain.reshape(1, DIFF_HV),
                lam_init=lam_init, tq=tq, tk=tk)
    return _out_stage(x, gate, o_f, o_b, z, o_d, dz, mg, md,
                      gla_norm_gain.reshape(1, GLA_HV), final_gain.reshape(1, D_MODEL),
                      wbg, wbd, wo, tile=row_tile)


def kernel(x_prompt, x_sample, c_prompt, c_sample, w_ada, b_ada, norm_gain, w_in, w_alpha, b_alpha,
           gla_norm_gain, lambda_q, lambda_k, diff_norm_gain, w_bo_gla, w_bo_diff, w_out, final_gain):
    depth = w_in.shape[0]
    assert depth == 1, "final RMSNorm is fused into the (single) layer's output stage"
    layer = 0
    lam_init = 0.8 - 0.6 * math.exp(-0.3 * layer)
    nb_p, nb_s = c_prompt.shape[0], c_sample.shape[0]
    rows = -(-(nb_p + nb_s) // 8) * 8
    c_all = jnp.concatenate(
        [c_prompt, c_sample, jnp.zeros((rows - nb_p - nb_s, D_MODEL), F32)], axis=0)
    mod = _ada(c_all, w_ada[layer].astype(BF16), b_ada[layer].reshape(1, 3 * D_MODEL))
    in_weights = _layer_weights(w_in[layer], w_alpha[layer], b_alpha[layer])
    common = dict(
        lam_init=lam_init, norm_gain=norm_gain[layer], in_weights=in_weights,
        gla_norm_gain=gla_norm_gain[layer], lambda_q=lambda_q[layer], lambda_k=lambda_k[layer],
        diff_norm_gain=diff_norm_gain[layer], wbg=w_bo_gla[layer].astype(BF16),
        wbd=w_bo_diff[layer].astype(BF16), wo=w_out[layer].astype(BF16), final_gain=final_gain)
    y_prompt = _trunk_layer(x_prompt, mod[:nb_p], row_tile=512, gla_tile=512, tq=512, tk=2048, **common)
    y_sample = _trunk_layer(x_sample, mod[nb_p:nb_p + nb_s], row_tile=512, gla_tile=512, tq=1024, tk=512,
                            **common)
    return (y_prompt, y_sample)
```

```python
import functools
import math

import numpy as np
import jax
import jax.numpy as jnp
from jax import lax
from jax.experimental import pallas as pl
from jax.experimental.pallas import tpu as pltpu

D_MODEL = 1024
GLA_HEADS = 4
GLA_DK = D_MODEL // 2
GLA_DV = D_MODEL
GLA_HK = GLA_DK // GLA_HEADS
GLA_HV = GLA_DV // GLA_HEADS
GLA_RANK = 16
GLA_LOGIT_NORM = 16.0
GLA_CHUNK = 64
GLA_SUB = 8
GLA_LEVELS = (1, 2, 4, 8, 16, 32)
DIFF_HEAD_DIM = 64
DIFF_HEADS = D_MODEL // (2 * DIFF_HEAD_DIM)
DIFF_HV = 2 * DIFF_HEAD_DIM
ROPE_THETA = 10000.0
EPS = 1e-6
LANES = 128
ONES_ROWS = 16
MXU_DEPTH = 256
ATTN_Q_SCALE = (DIFF_HEAD_DIM ** -0.5) * math.log2(math.e)

F32 = jnp.float32
BF16 = jnp.bfloat16

VMEM_LIMIT_BYTES = 56 * 1024 * 1024

NT_DIMS = (((1,), (1,)), ((), ()))
TN_DIMS = (((0,), (0,)), ((), ()))


def _sigmoid(x):
    return 1.0 / (1.0 + jnp.exp(-x))


def _const_spec(shape):
    nd = len(shape)
    return pl.BlockSpec(shape, lambda *_: (0,) * nd, pipeline_mode=pl.Buffered(1))


def _ada_kernel(c_ref, w_ref, b_ref, o_ref):
    c = c_ref[...]
    s = c * _sigmoid(c)
    o_ref[...] = jnp.dot(s.astype(BF16), w_ref[...], preferred_element_type=F32) + b_ref[...]


def _ada(c, w_bf16, b):
    rows = c.shape[0]
    n_out = w_bf16.shape[1]
    tn = D_MODEL
    return pl.pallas_call(
        _ada_kernel,
        grid=(n_out // tn,),
        in_specs=[
            pl.BlockSpec((rows, D_MODEL), lambda j: (0, 0)),
            pl.BlockSpec((D_MODEL, tn), lambda j: (0, j)),
            pl.BlockSpec((1, tn), lambda j: (0, j)),
        ],
        out_specs=pl.BlockSpec((rows, tn), lambda j: (0, j)),
        out_shape=jax.ShapeDtypeStruct((rows, n_out), F32),
        name="ada",
    )(c, w_bf16, b)


def _rope_slices(x, cs, sn):
    lane = lax.broadcasted_iota(jnp.int32, (x.shape[0], LANES), 1)
    is_x2 = (lane & 32) != 0
    outs = []
    for s in range(x.shape[1] // LANES):
        xs = x[:, s * LANES:(s + 1) * LANES]
        from_lo = pltpu.roll(xs, 32, axis=1)
        from_hi = pltpu.roll(xs, LANES - 32, axis=1)
        partner = jnp.where(is_x2, from_lo, from_hi)
        outs.append(xs * cs + partner * sn)
    return jnp.concatenate(outs, axis=1)


def _inproj_kernel(x_ref, shift_ref, scale_ref, gain_ref, cs_ref, sn_ref, trif_ref, trib_ref,
                   wg_ref, wlow_ref, walpha_ref, balpha_ref,
                   wdq_ref, wdk_ref, wdv_ref, wdz_ref, wmg_ref, wmd_ref,
                   q_out, k_out, v_out, z_out, b_out,
                   dq_out, dk_out, dv_out, dz_out, mg_out, md_out):
    x = x_ref[0]
    ms = jnp.mean(x * x, axis=-1, keepdims=True)
    y = x * lax.rsqrt(ms + EPS) * gain_ref[...]
    h = (y * (1.0 + scale_ref[0]) + shift_ref[0]).astype(BF16)

    pg = jnp.dot(h, wg_ref[...], preferred_element_type=F32)
    q_out[0] = (pg[:, 0:GLA_DK] * (GLA_HK ** -0.5)).astype(BF16)
    k_out[0] = pg[:, GLA_DK:2 * GLA_DK].astype(BF16)
    v_out[0] = pg[:, 2 * GLA_DK:2 * GLA_DK + GLA_DV].astype(BF16)
    az = pg[:, 2 * GLA_DK + GLA_DV:]
    z_out[0] = (az * _sigmoid(az)).astype(BF16)

    low = jnp.dot(h, wlow_ref[...], preferred_element_type=F32)
    logits = jnp.dot(low.astype(BF16), walpha_ref[...], preferred_element_type=F32) + balpha_ref[...]
    g = (jnp.minimum(logits, 0.0) - jnp.log1p(jnp.exp(-jnp.abs(logits)))) * (1.0 / GLA_LOGIT_NORM)
    g_hi = g.astype(BF16)
    g_lo = (g - g_hi.astype(F32)).astype(BF16)
    tr = trif_ref.shape[0]
    for tri_ref, lo in ((trif_ref, 0), (trib_ref, GLA_DK)):
        tri = tri_ref[...]
        for r in range(0, g.shape[0], tr):
            b_out[0, r:r + tr, lo:lo + GLA_DK] = (
                jnp.dot(tri, g_hi[r:r + tr, lo:lo + GLA_DK], preferred_element_type=F32)
                + jnp.dot(tri, g_lo[r:r + tr, lo:lo + GLA_DK], preferred_element_type=F32))

    cs = cs_ref[...]
    sn = sn_ref[...]
    dq = jnp.dot(h, wdq_ref[...], preferred_element_type=F32)
    dq_out[0] = (_rope_slices(dq, cs, sn) * ATTN_Q_SCALE).astype(BF16)
    dk = jnp.dot(h, wdk_ref[...], preferred_element_type=F32)
    dk_out[0] = _rope_slices(dk, cs, sn).astype(BF16)
    dv_out[0] = lax.dot_general(wdv_ref[...], h, NT_DIMS, preferred_element_type=F32).astype(BF16)
    dz = jnp.dot(h, wdz_ref[...], preferred_element_type=F32)
    dz_out[0] = (dz * _sigmoid(dz)).astype(BF16)
    mg_out[0] = _sigmoid(jnp.dot(h, wmg_ref[...], preferred_element_type=F32)).astype(BF16)
    md_out[0] = _sigmoid(jnp.dot(h, wmd_ref[...], preferred_element_type=F32)).astype(BF16)


def _chunk_tri(tile, reverse):
    i = np.arange(tile)[:, None]
    j = np.arange(tile)[None, :]
    same = (i // GLA_CHUNK) == (j // GLA_CHUNK)
    tri = (j >= i) if reverse else (j <= i)
    return jnp.asarray(same & tri, dtype=BF16)


def _inproj(x, shift, scale, gain, cs, sn, weights, tile):
    bsz, seq, _ = x.shape
    (wg, wlow, walpha, balpha, wdq, wdk, wdv, wdz, wmg, wmd) = weights
    tri_rows = min(tile, MXU_DEPTH)
    trif = _chunk_tri(tri_rows, False)
    trib = _chunk_tri(tri_rows, True)
    row = lambda c: pl.BlockSpec((1, tile, c), lambda b, i: (b, i, 0))
    per_b = pl.BlockSpec((1, 1, D_MODEL), lambda b, i: (b, 0, 0))
    consts = [gain, None, None, trif, trib, wg, wlow, walpha, balpha, wdq, wdk, wdv, wdz, wmg, wmd]
    in_specs = [row(D_MODEL), per_b, per_b]
    for a in consts:
        if a is None:
            in_specs.append(pl.BlockSpec((tile, LANES), lambda b, i: (i, 0)))
        else:
            in_specs.append(_const_spec(a.shape))
    out_cols = [(GLA_DK, BF16), (GLA_DK, BF16), (GLA_DV, BF16), (GLA_DV, BF16), (2 * GLA_DK, F32)] + \
               [(D_MODEL, BF16)] * 6
    out_specs = [row(c) for c, _ in out_cols]
    out_shape = [jax.ShapeDtypeStruct((bsz, seq, c), dt) for c, dt in out_cols]
    dv_index = 7
    out_specs[dv_index] = pl.BlockSpec((1, D_MODEL, tile), lambda b, i: (b, 0, i))
    out_shape[dv_index] = jax.ShapeDtypeStruct((bsz, D_MODEL, seq), BF16)
    return pl.pallas_call(
        _inproj_kernel,
        grid=(bsz, seq // tile),
        in_specs=in_specs,
        out_specs=out_specs,
        out_shape=out_shape,
        compiler_params=pltpu.CompilerParams(
            dimension_semantics=("parallel", "parallel"), vmem_limit_bytes=VMEM_LIMIT_BYTES),
        name="inproj",
    )(x, shift, scale, gain, cs, sn, trif, trib, wg, wlow, walpha, balpha, wdq, wdk, wdv, wdz, wmg, wmd)


def _gla_masks(reverse):
    c = GLA_CHUNK
    i = np.arange(c)[:, None]
    j = np.arange(c)[None, :]
    q_half, k_half = (0, 1) if reverse else (1, 0)
    masks = []
    for s in GLA_LEVELS:
        same_parent = (i // (2 * s)) == (j // (2 * s))
        masks.append(same_parent & ((i // s) % 2 == q_half) & ((j // s) % 2 == k_half))
    masks.append(i == j)
    return jnp.asarray(np.stack(masks).astype(np.float32))


def _gla_kernel(qf_ref, kf_ref, vf_ref, bf_ref, qb_ref, kb_ref, vb_ref, bb_ref, mask_ref,
                of_ref, ob_ref, state_ref, b32_ref, *, tile):
    c = GLA_CHUNK
    n_chunks = tile // c
    refs = {False: (qf_ref, kf_ref, vf_ref, bf_ref, of_ref), True: (qb_ref, kb_ref, vb_ref, bb_ref, ob_ref)}
    chains = [(reverse, h) for reverse in (False, True) for h in range(GLA_HEADS)]

    @pl.when(pl.program_id(1) == 0)
    def _():
        state_ref[...] = jnp.zeros_like(state_ref)

    sub8 = lax.broadcasted_iota(jnp.int32, (GLA_SUB, GLA_HK), 0)

    def chunk_body(ci, carry):
        row0 = {False: pl.multiple_of(ci * c, c), True: pl.multiple_of((n_chunks - 1 - ci) * c, c)}
        staged = []
        for reverse, h in chains:
            q_ref, k_ref, v_ref, b_ref, o_ref = refs[reverse]
            d = int(reverse)
            rows = pl.ds(row0[reverse], c)
            scan_last = 0 if reverse else c - 1
            lk = slice(h * GLA_HK, (h + 1) * GLA_HK)
            lv = slice(h * GLA_HV, (h + 1) * GLA_HV)
            q = q_ref[0, rows, lk].astype(F32)
            k = k_ref[0, rows, lk].astype(F32)
            v = v_ref[0, rows, lv]
            b = b_ref[0, rows, lk]
            b32_ref[d, h] = b

            def b_row(r, d=d, h=h):
                return b32_ref[d, h, pl.ds(r, 1), :]

            b_tot = b_row(scan_last)

            state = state_ref[d, h]
            q_in = (q * jnp.exp(b)).astype(BF16)
            o_inter = jnp.dot(q_in, state.astype(BF16), preferred_element_type=F32)

            diag = jnp.sum(q * k, axis=1, keepdims=True)
            level_scores = []
            for s in GLA_LEVELS:
                off = s if reverse else s - 1
                blocks = []
                for jb in range(c // GLA_SUB):
                    r0 = jb * GLA_SUB
                    if 2 * s >= GLA_SUB:
                        blk = jnp.broadcast_to(b_row(r0 // (2 * s) * (2 * s) + off), (GLA_SUB, GLA_HK))
                    else:
                        blk = jnp.broadcast_to(b_row(r0 + off), (GLA_SUB, GLA_HK))
                        for t in range(1, GLA_SUB // (2 * s)):
                            blk = jnp.where(sub8 >= 2 * s * t, b_row(r0 + 2 * s * t + off), blk)
                    blocks.append(blk)
                ref = jnp.concatenate(blocks, axis=0)
                e = jnp.exp(-jnp.abs(b - ref))
                level_scores.append(lax.dot_general(
                    (q * e).astype(BF16), (k * e).astype(BF16), NT_DIMS, preferred_element_type=F32))

            k_hat = (k * jnp.exp(b_tot - b)).astype(BF16)
            delta = lax.dot_general(k_hat, v, TN_DIMS, preferred_element_type=F32)
            decay_col = jnp.transpose(jnp.broadcast_to(jnp.exp(b_tot), (GLA_HK, GLA_HK)))
            decay = jnp.concatenate([decay_col] * (GLA_HV // GLA_HK), axis=1)
            staged.append((d, h, o_ref, rows, lv, v, o_inter, diag, level_scores, state * decay + delta))

        for d, h, o_ref, rows, lv, v, o_inter, diag, level_scores, new_state in staged:
            scores = mask_ref[d, len(GLA_LEVELS)] * diag
            for lvl, ls in enumerate(level_scores):
                scores = scores + mask_ref[d, lvl] * ls
            o = o_inter + jnp.dot(scores.astype(BF16), v, preferred_element_type=F32)
            o_ref[0, rows, lv] = o.astype(o_ref.dtype)
            state_ref[d, h] = new_state
        return carry

    lax.fori_loop(0, n_chunks, chunk_body, 0)


def _gla(q, k, v, bcum, *, tile):
    bsz, seq, _ = q.shape
    nt = seq // tile
    masks = jnp.stack([_gla_masks(False), _gla_masks(True)])
    fwd = lambda b, t: (b, t, 0)
    bwd = lambda b, t: (b, nt - 1 - t, 0)
    bwd_b = lambda b, t: (b, nt - 1 - t, 1)
    kernel = functools.partial(_gla_kernel, tile=tile)
    out = jax.ShapeDtypeStruct((bsz, seq, GLA_DV), BF16)
    return pl.pallas_call(
        kernel,
        grid=(bsz, nt),
        in_specs=[
            pl.BlockSpec((1, tile, GLA_DK), fwd), pl.BlockSpec((1, tile, GLA_DK), fwd),
            pl.BlockSpec((1, tile, GLA_DV), fwd), pl.BlockSpec((1, tile, GLA_DK), fwd),
            pl.BlockSpec((1, tile, GLA_DK), bwd), pl.BlockSpec((1, tile, GLA_DK), bwd),
            pl.BlockSpec((1, tile, GLA_DV), bwd), pl.BlockSpec((1, tile, GLA_DK), bwd_b),
            _const_spec(masks.shape),
        ],
        out_specs=[pl.BlockSpec((1, tile, GLA_DV), fwd), pl.BlockSpec((1, tile, GLA_DV), bwd)],
        out_shape=[out, out],
        scratch_shapes=[
            pltpu.VMEM((2, GLA_HEADS, GLA_HK, GLA_HV), F32),
            pltpu.VMEM((2, GLA_HEADS, GLA_CHUNK, GLA_HK), F32),
        ],
        compiler_params=pltpu.CompilerParams(
            dimension_semantics=("parallel", "arbitrary"), vmem_limit_bytes=VMEM_LIMIT_BYTES),
        name="gla",
    )(q, k, v, bcum, q, k, v, bcum, masks)


def _attn_kernel(q_ref, k_ref, vt_ref, lq_ref, lk_ref, gain_ref, o_ref,
                 q2_ref, s_ref, smax_ref, m_ref, acc_ref, *, tq, tk, seq, lam_init):
    n_tiles = seq // tk
    n_q = seq // tq
    kc = min(MXU_DEPTH, tk)
    n_kc = tk // kc
    ones_rows = jnp.ones((ONES_ROWS, kc), BF16)
    lane = lax.broadcasted_iota(jnp.int32, (tq, LANES), 1)

    def stage_queries(qi):
        q = q_ref[0, pl.ds(pl.multiple_of(qi * tq, tq), tq), :]
        zero = jnp.zeros_like(q)
        q2_ref[0:tq] = jnp.where(lane < DIFF_HEAD_DIM, q, zero)
        q2_ref[tq:2 * tq] = jnp.where(lane >= DIFF_HEAD_DIM, q, zero)

    def reset_stats():
        m_ref[...] = jnp.full_like(m_ref, -jnp.inf)
        acc_ref[...] = jnp.zeros_like(acc_ref)

    def scores_chunk(kt, slot, c, smax):
        ks = pl.multiple_of(kt * tk, tk)
        k = k_ref[0, pl.ds(ks + c * kc, kc), :]
        s = lax.dot_general(k, q2_ref[...], NT_DIMS, preferred_element_type=F32)
        s_ref[slot, c * kc:(c + 1) * kc, :] = s
        cmax = jnp.max(s, axis=0, keepdims=True)
        return cmax if smax is None else jnp.maximum(smax, cmax)

    def values_chunk(kt, slot, c, m_new, pv):
        ks = pl.multiple_of(kt * tk, tk)
        vt = jnp.concatenate([vt_ref[0, :, pl.ds(ks + c * kc, kc)], ones_rows], axis=0)
        p = jnp.exp2(s_ref[slot, c * kc:(c + 1) * kc, :] - m_new).astype(BF16)
        d = jnp.dot(vt, p, preferred_element_type=F32)
        return d if pv is None else pv + d

    def step(i, slot, next_kt):
        m_prev = m_ref[...]
        m_new = jnp.maximum(m_prev, smax_ref[slot])
        alpha = jnp.exp2(m_prev - m_new)
        m_ref[...] = m_new
        smax, pv = None, None
        for c in range(n_kc):
            smax = scores_chunk(next_kt, 1 - slot, c, smax)
            pv = values_chunk(i, slot, c, m_new, pv)
        smax_ref[1 - slot] = smax
        acc_ref[...] = alpha * acc_ref[...] + pv

    dots = jnp.sum(lq_ref[...] * lk_ref[...], axis=1, keepdims=True)
    e = jnp.exp(dots)
    lam = e[0:1, :] - e[1:2, :] + lam_init

    def finalize(qi):
        o2 = acc_ref[0:DIFF_HV] / acc_ref[DIFF_HV:DIFF_HV + 1]
        ot = o2[:, 0:tq] - lam * o2[:, tq:2 * tq]
        ms = jnp.mean(ot * ot, axis=0, keepdims=True)
        on = jnp.transpose(ot * lax.rsqrt(ms + EPS))
        o_ref[0, pl.ds(pl.multiple_of(qi * tq, tq), tq), :] = (
            on * gain_ref[...] * (1.0 - lam_init)).astype(o_ref.dtype)

    stage_queries(0)
    reset_stats()
    smax0 = None
    for c in range(n_kc):
        smax0 = scores_chunk(0, 0, c, smax0)
    smax_ref[0] = smax0

    def query_tile(qi, carry):
        def pair(j, c2):
            i = 2 * j
            step(i, 0, i + 1)
            step(i + 1, 1, i + 2)
            return c2

        lax.fori_loop(0, (n_tiles - 2) // 2, pair, 0)
        step(n_tiles - 2, 0, n_tiles - 1)
        stage_queries(jnp.minimum(qi + 1, n_q - 1))
        step(n_tiles - 1, 1, 0)
        finalize(qi)
        reset_stats()
        return carry

    lax.fori_loop(0, n_q, query_tile, 0)


def _attn(dq, dk, dvt, lq, lk, gain, *, lam_init, tq, tk):
    bsz, seq, _ = dq.shape
    assert seq % (2 * tk) == 0 and seq % tq == 0
    kernel = functools.partial(_attn_kernel, tq=tq, tk=tk, seq=seq, lam_init=lam_init)
    head_rows = pl.BlockSpec((1, seq, DIFF_HV), lambda b, h: (b, 0, h))
    return pl.pallas_call(
        kernel,
        grid=(bsz, DIFF_HEADS),
        in_specs=[
            head_rows, head_rows,
            pl.BlockSpec((1, DIFF_HV, seq), lambda b, h: (b, h, 0)),
            pl.BlockSpec((2, DIFF_HEAD_DIM), lambda b, h: (0, 0)),
            pl.BlockSpec((2, DIFF_HEAD_DIM), lambda b, h: (0, 0)),
            pl.BlockSpec((1, DIFF_HV), lambda b, h: (0, 0)),
        ],
        out_specs=head_rows,
        out_shape=jax.ShapeDtypeStruct((bsz, seq, D_MODEL), BF16),
        scratch_shapes=[
            pltpu.VMEM((2 * tq, DIFF_HV), BF16),
            pltpu.VMEM((2, tk, 2 * tq), F32),
            pltpu.VMEM((2, 1, 2 * tq), F32),
            pltpu.VMEM((1, 2 * tq), F32),
            pltpu.VMEM((DIFF_HV + ONES_ROWS, 2 * tq), F32),
        ],
        compiler_params=pltpu.CompilerParams(
            dimension_semantics=("parallel", "parallel"),
            vmem_limit_bytes=VMEM_LIMIT_BYTES),
        name="diff_attn",
    )(dq, dk, dvt, lq, lk, gain)


def _out_kernel(x_ref, gate_ref, of_ref, ob_ref, z_ref, od_ref, dz_ref, mg_ref, md_ref,
                ggain_ref, fgain_ref, wbg_ref, wbd_ref, wo_ref, y_ref):
    og = of_ref[0].astype(F32) + ob_ref[0].astype(F32)
    ggain = ggain_ref[...]
    heads = []
    for h in range(GLA_HEADS):
        oh = og[:, h * GLA_HV:(h + 1) * GLA_HV]
        ms = jnp.mean(oh * oh, axis=1, keepdims=True)
        heads.append(oh * lax.rsqrt(ms + EPS) * ggain)
    og = jnp.concatenate(heads, axis=1) * z_ref[0].astype(F32)
    y_gla = jnp.dot(og.astype(BF16), wbg_ref[...], preferred_element_type=F32)
    od = od_ref[0].astype(F32) * dz_ref[0].astype(F32)
    y_diff = jnp.dot(od.astype(BF16), wbd_ref[...], preferred_element_type=F32)
    merged = mg_ref[0].astype(F32) * y_gla + md_ref[0].astype(F32) * y_diff
    y = jnp.dot(merged.astype(BF16), wo_ref[...], preferred_element_type=F32)
    r = x_ref[0] + gate_ref[0] * y
    ms = jnp.mean(r * r, axis=1, keepdims=True)
    y_ref[0] = r * lax.rsqrt(ms + EPS) * fgain_ref[...]


def _out_stage(x, gate, o_f, o_b, z, o_d, dz, mg, md, ggain, fgain, wbg, wbd, wo, *, tile):
    bsz, seq, _ = x.shape
    row = pl.BlockSpec((1, tile, D_MODEL), lambda b, i: (b, i, 0))
    per_b = pl.BlockSpec((1, 1, D_MODEL), lambda b, i: (b, 0, 0))
    return pl.pallas_call(
        _out_kernel,
        grid=(bsz, seq // tile),
        in_specs=[row, per_b] + [row] * 7 + [
            _const_spec(ggain.shape), _const_spec(fgain.shape),
            _const_spec(wbg.shape), _const_spec(wbd.shape), _const_spec(wo.shape)],
        out_specs=row,
        out_shape=jax.ShapeDtypeStruct((bsz, seq, D_MODEL), F32),
        compiler_params=pltpu.CompilerParams(
            dimension_semantics=("parallel", "parallel"), vmem_limit_bytes=VMEM_LIMIT_BYTES),
        name="out_stage",
    )(x, gate, o_f, o_b, z, o_d, dz, mg, md, ggain, fgain, wbg, wbd, wo)


def _rope_tables(seq):
    half = DIFF_HEAD_DIM // 2
    inv = 1.0 / (ROPE_THETA ** (jnp.arange(half, dtype=F32) / half))
    ang = jnp.arange(seq, dtype=F32)[:, None] * inv[None, :]
    cos = jnp.cos(ang)
    sin = jnp.sin(ang)
    cs = jnp.concatenate([cos, cos, cos, cos], axis=1)
    sn = jnp.concatenate([-sin, sin, -sin, sin], axis=1)
    return cs, sn


def _layer_weights(w_in, w_alpha, b_alpha):
    splits = (GLA_DK, GLA_DK, GLA_DV, GLA_DV, 2 * GLA_RANK) + (D_MODEL,) * 6
    pts = np.cumsum(splits)[:-1].tolist()
    a_q, a_k, a_v, a_z, a_low, d_q, d_k, d_v, d_z, m_g, m_d = jnp.split(w_in, pts, axis=1)
    wg = jnp.concatenate([a_q, a_k, a_v, a_z], axis=1).astype(BF16)
    zeros = jnp.zeros((GLA_RANK, GLA_DK), w_alpha.dtype)
    walpha = jnp.concatenate([
        jnp.concatenate([w_alpha[0], zeros], axis=1),
        jnp.concatenate([zeros, w_alpha[1]], axis=1)], axis=0).astype(BF16)
    balpha = b_alpha.reshape(1, 2 * GLA_DK)
    return (wg, a_low.astype(BF16), walpha, balpha,
            d_q.astype(BF16), d_k.astype(BF16), d_v.T.astype(BF16), d_z.astype(BF16),
            m_g.astype(BF16), m_d.astype(BF16))


def _trunk_layer(x, mod, lam_init, norm_gain, in_weights, gla_norm_gain, lambda_q, lambda_k,
                 diff_norm_gain, wbg, wbd, wo, final_gain, *, row_tile, gla_tile, tq, tk):
    bsz, seq, _ = x.shape
    row_tile, gla_tile, tq, tk = (min(t, s) for t, s in
                                  ((row_tile, seq), (gla_tile, seq), (tq, seq), (tk, seq // 2)))
    shift, scale, gate = [m.reshape(bsz, 1, D_MODEL) for m in jnp.split(mod, 3, axis=-1)]
    cs, sn = _rope_tables(seq)
    (q, k, v, z, bcum, dq, dk, dv, dz, mg, md) = _inproj(
        x, shift, scale, norm_gain.reshape(1, D_MODEL), cs, sn, in_weights, row_tile)
    o_f, o_b = _gla(q, k, v, bcum, tile=gla_tile)
    o_d = _attn(dq, dk, dv, lambda_q, lambda_k, diff_norm_gain.reshape(1, DIFF_HV),
                lam_init=lam_init, tq=tq, tk=tk)
    return _out_stage(x, gate, o_f, o_b, z, o_d, dz, mg, md,
                      gla_norm_gain.reshape(1, GLA_HV), final_gain.reshape(1, D_MODEL),
                      wbg, wbd, wo, tile=row_tile)


def kernel(x_prompt, x_sample, c_prompt, c_sample, w_ada, b_ada, norm_gain, w_in, w_alpha, b_alpha,
           gla_norm_gain, lambda_q, lambda_k, diff_norm_gain, w_bo_gla, w_bo_diff, w_out, final_gain):
    depth = w_in.shape[0]
    assert depth == 1, "final RMSNorm is fused into the (single) layer's output stage"
    layer = 0
    lam_init = 0.8 - 0.6 * math.exp(-0.3 * layer)
    nb_p, nb_s = c_prompt.shape[0], c_sample.shape[0]
    rows = -(-(nb_p + nb_s) // 8) * 8
    c_all = jnp.concatenate(
        [c_prompt, c_sample, jnp.zeros((rows - nb_p - nb_s, D_MODEL), F32)], axis=0)
    mod = _ada(c_all, w_ada[layer].astype(BF16), b_ada[layer].reshape(1, 3 * D_MODEL))
    in_weights = _layer_weights(w_in[layer], w_alpha[layer], b_alpha[layer])
    common = dict(
        lam_init=lam_init, norm_gain=norm_gain[layer], in_weights=in_weights,
        gla_norm_gain=gla_norm_gain[layer], lambda_q=lambda_q[layer], lambda_k=lambda_k[layer],
        diff_norm_gain=diff_norm_gain[layer], wbg=w_bo_gla[layer].astype(BF16),
        wbd=w_bo_diff[layer].astype(BF16), wo=w_out[layer].astype(BF16), final_gain=final_gain)
    y_prompt = _trunk_layer(x_prompt, mod[:nb_p], row_tile=512, gla_tile=512, tq=512, tk=2048, **common)
    y_sample = _trunk_layer(x_sample, mod[nb_p:nb_p + nb_s], row_tile=512, gla_tile=512, tq=1024, tk=512,
                            **common)
    return (y_prompt, y_sample)
```

```python
import functools
import math

import numpy as np
import jax
import jax.numpy as jnp
from jax import lax
from jax.experimental import pallas as pl
from jax.experimental.pallas import tpu as pltpu

D_MODEL = 1024
GLA_HEADS = 4
GLA_DK = D_MODEL // 2
GLA_DV = D_MODEL
GLA_HK = GLA_DK // GLA_HEADS
GLA_HV = GLA_DV // GLA_HEADS
GLA_RANK = 16
GLA_LOGIT_NORM = 16.0
GLA_CHUNK = 64
GLA_SUB = 8
GLA_LEVELS = (1, 2, 4, 8, 16, 32)
DIFF_HEAD_DIM = 64
DIFF_HEADS = D_MODEL // (2 * DIFF_HEAD_DIM)
DIFF_HV = 2 * DIFF_HEAD_DIM
ROPE_THETA = 10000.0
EPS = 1e-6
LANES = 128
ONES_ROWS = 16
MXU_DEPTH = 256
ATTN_Q_SCALE = (DIFF_HEAD_DIM ** -0.5) * math.log2(math.e)

F32 = jnp.float32
BF16 = jnp.bfloat16

VMEM_LIMIT_BYTES = 56 * 1024 * 1024

NT_DIMS = (((1,), (1,)), ((), ()))
TN_DIMS = (((0,), (0,)), ((), ()))


def _sigmoid(x):
    return 1.0 / (1.0 + jnp.exp(-x))


def _const_spec(shape):
    nd = len(shape)
    return pl.BlockSpec(shape, lambda *_: (0,) * nd, pipeline_mode=pl.Buffered(1))


def _ada_kernel(c_ref, w_ref, b_ref, o_ref):
    c = c_ref[...]
    s = c * _sigmoid(c)
    o_ref[...] = jnp.dot(s.astype(BF16), w_ref[...], preferred_element_type=F32) + b_ref[...]


def _ada(c, w_bf16, b):
    rows = c.shape[0]
    n_out = w_bf16.shape[1]
    tn = D_MODEL
    return pl.pallas_call(
        _ada_kernel,
        grid=(n_out // tn,),
        in_specs=[
            pl.BlockSpec((rows, D_MODEL), lambda j: (0, 0)),
            pl.BlockSpec((D_MODEL, tn), lambda j: (0, j)),
            pl.BlockSpec((1, tn), lambda j: (0, j)),
        ],
        out_specs=pl.BlockSpec((rows, tn), lambda j: (0, j)),
        out_shape=jax.ShapeDtypeStruct((rows, n_out), F32),
        name="ada",
    )(c, w_bf16, b)


def _rope_slices(x, cs, sn):
    lane = lax.broadcasted_iota(jnp.int32, (x.shape[0], LANES), 1)
    is_x2 = (lane & 32) != 0
    outs = []
    for s in range(x.shape[1] // LANES):
        xs = x[:, s * LANES:(s + 1) * LANES]
        from_lo = pltpu.roll(xs, 32, axis=1)
        from_hi = pltpu.roll(xs, LANES - 32, axis=1)
        partner = jnp.where(is_x2, from_lo, from_hi)
        outs.append(xs * cs + partner * sn)
    return jnp.concatenate(outs, axis=1)


def _inproj_kernel(x_ref, shift_ref, scale_ref, gain_ref, cs_ref, sn_ref, trif_ref, trib_ref,
                   wg_ref, wlow_ref, walpha_ref, balpha_ref,
                   wdq_ref, wdk_ref, wdv_ref, wdz_ref, wmg_ref, wmd_ref,
                   q_out, k_out, v_out, z_out, b_out,
                   dq_out, dk_out, dv_out, dz_out, mg_out, md_out):
    x = x_ref[0]
    ms = jnp.mean(x * x, axis=-1, keepdims=True)
    y = x * lax.rsqrt(ms + EPS) * gain_ref[...]
    h = (y * (1.0 + scale_ref[0]) + shift_ref[0]).astype(BF16)

    pg = jnp.dot(h, wg_ref[...], preferred_element_type=F32)
    q_out[0] = (pg[:, 0:GLA_DK] * (GLA_HK ** -0.5)).astype(BF16)
    k_out[0] = pg[:, GLA_DK:2 * GLA_DK].astype(BF16)
    v_out[0] = pg[:, 2 * GLA_DK:2 * GLA_DK + GLA_DV].astype(BF16)
    az = pg[:, 2 * GLA_DK + GLA_DV:]
    z_out[0] = (az * _sigmoid(az)).astype(BF16)

    low = jnp.dot(h, wlow_ref[...], preferred_element_type=F32)
    logits = jnp.dot(low.astype(BF16), walpha_ref[...], preferred_element_type=F32) + balpha_ref[...]
    g = (jnp.minimum(logits, 0.0) - jnp.log1p(jnp.exp(-jnp.abs(logits)))) * (1.0 / GLA_LOGIT_NORM)
    g_hi = g.astype(BF16)
    g_lo = (g - g_hi.astype(F32)).astype(BF16)
    tr = trif_ref.shape[0]
    for tri_ref, lo in ((trif_ref, 0), (trib_ref, GLA_DK)):
        tri = tri_ref[...]
        for r in range(0, g.shape[0], tr):
            b_out[0, r:r + tr, lo:lo + GLA_DK] = (
                jnp.dot(tri, g_hi[r:r + tr, lo:lo + GLA_DK], preferred_element_type=F32)
                + jnp.dot(tri, g_lo[r:r + tr, lo:lo + GLA_DK], preferred_element_type=F32))

    cs = cs_ref[...]
    sn = sn_ref[...]
    dq = jnp.dot(h, wdq_ref[...], preferred_element_type=F32)
    dq_out[0] = (_rope_slices(dq, cs, sn) * ATTN_Q_SCALE).astype(BF16)
    dk = jnp.dot(h, wdk_ref[...], preferred_element_type=F32)
    dk_out[0] = _rope_slices(dk, cs, sn).astype(BF16)
    dv_out[0] = lax.dot_general(wdv_ref[...], h, NT_DIMS, preferred_element_type=F32).astype(BF16)
    dz = jnp.dot(h, wdz_ref[...], preferred_element_type=F32)
    dz_out[0] = (dz * _sigmoid(dz)).astype(BF16)
    mg_out[0] = _sigmoid(jnp.dot(h, wmg_ref[...], preferred_element_type=F32)).astype(BF16)
    md_out[0] = _sigmoid(jnp.dot(h, wmd_ref[...], preferred_element_type=F32)).astype(BF16)


def _chunk_tri(tile, reverse):
    i = np.arange(tile)[:, None]
    j = np.arange(tile)[None, :]
    same = (i // GLA_CHUNK) == (j // GLA_CHUNK)
    tri = (j >= i) if reverse else (j <= i)
    return jnp.asarray(same & tri, dtype=BF16)


def _inproj(x, shift, scale, gain, cs, sn, weights, tile):
    bsz, seq, _ = x.shape
    (wg, wlow, walpha, balpha, wdq, wdk, wdv, wdz, wmg, wmd) = weights
    tri_rows = min(tile, MXU_DEPTH)
    trif = _chunk_tri(tri_rows, False)
    trib = _chunk_tri(tri_rows, True)
    row = lambda c: pl.BlockSpec((1, tile, c), lambda b, i: (b, i, 0))
    per_b = pl.BlockSpec((1, 1, D_MODEL), lambda b, i: (b, 0, 0))
    consts = [gain, None, None, trif, trib, wg, wlow, walpha, balpha, wdq, wdk, wdv, wdz, wmg, wmd]
    in_specs = [row(D_MODEL), per_b, per_b]
    for a in consts:
        if a is None:
            in_specs.append(pl.BlockSpec((tile, LANES), lambda b, i: (i, 0)))
        else:
            in_specs.append(_const_spec(a.shape))
    out_cols = [(GLA_DK, BF16), (GLA_DK, BF16), (GLA_DV, BF16), (GLA_DV, BF16), (2 * GLA_DK, F32)] + \
               [(D_MODEL, BF16)] * 6
    out_specs = [row(c) for c, _ in out_cols]
    out_shape = [jax.ShapeDtypeStruct((bsz, seq, c), dt) for c, dt in out_cols]
    dv_index = 7
    out_specs[dv_index] = pl.BlockSpec((1, D_MODEL, tile), lambda b, i: (b, 0, i))
    out_shape[dv_index] = jax.ShapeDtypeStruct((bsz, D_MODEL, seq), BF16)
    return pl.pallas_call(
        _inproj_kernel,
        grid=(bsz, seq // tile),
        in_specs=in_specs,
        out_specs=out_specs,
        out_shape=out_shape,
        compiler_params=pltpu.CompilerParams(
            dimension_semantics=("parallel", "parallel"), vmem_limit_bytes=VMEM_LIMIT_BYTES),
        name="inproj",
    )(x, shift, scale, gain, cs, sn, trif, trib, wg, wlow, walpha, balpha, wdq, wdk, wdv, wdz, wmg, wmd)


def _gla_masks(reverse):
    c = GLA_CHUNK
    i = np.arange(c)[:, None]
    j = np.arange(c)[None, :]
    q_half, k_half = (0, 1) if reverse else (1, 0)
    masks = []
    for s in GLA_LEVELS:
        same_parent = (i // (2 * s)) == (j // (2 * s))
        masks.append(same_parent & ((i // s) % 2 == q_half) & ((j // s) % 2 == k_half))
    masks.append(i == j)
    return jnp.asarray(np.stack(masks).astype(np.float32))


def _gla_kernel(qf_ref, kf_ref, vf_ref, bf_ref, qb_ref, kb_ref, vb_ref, bb_ref, mask_ref,
                of_ref, ob_ref, state_ref, b32_ref, *, tile):
    c = GLA_CHUNK
    n_chunks = tile // c
    refs = {False: (qf_ref, kf_ref, vf_ref, bf_ref, of_ref), True: (qb_ref, kb_ref, vb_ref, bb_ref, ob_ref)}
    chains = [(reverse, h) for reverse in (False, True) for h in range(GLA_HEADS)]

    @pl.when(pl.program_id(1) == 0)
    def _():
        state_ref[...] = jnp.zeros_like(state_ref)

    sub8 = lax.broadcasted_iota(jnp.int32, (GLA_SUB, GLA_HK), 0)

    def chunk_body(ci, carry):
        row0 = {False: pl.multiple_of(ci * c, c), True: pl.multiple_of((n_chunks - 1 - ci) * c, c)}
        staged = []
        for reverse, h in chains:
            q_ref, k_ref, v_ref, b_ref, o_ref = refs[reverse]
            d = int(reverse)
            rows = pl.ds(row0[reverse], c)
            scan_last = 0 if reverse else c - 1
            lk = slice(h * GLA_HK, (h + 1) * GLA_HK)
            lv = slice(h * GLA_HV, (h + 1) * GLA_HV)
            q = q_ref[0, rows, lk].astype(F32)
            k = k_ref[0, rows, lk].astype(F32)
            v = v_ref[0, rows, lv]
            b = b_ref[0, rows, lk]
            b32_ref[d, h] = b

            def b_row(r, d=d, h=h):
                return b32_ref[d, h, pl.ds(r, 1), :]

            b_tot = b_row(scan_last)

            state = state_ref[d, h]
            q_in = (q * jnp.exp(b)).astype(BF16)
            o_inter = jnp.dot(q_in, state.astype(BF16), preferred_element_type=F32)

            diag = jnp.sum(q * k, axis=1, keepdims=True)
            level_scores = []
            for s in GLA_LEVELS:
                off = s if reverse else s - 1
                blocks = []
                for jb in range(c // GLA_SUB):
                    r0 = jb * GLA_SUB
                    if 2 * s >= GLA_SUB:
                        blk = jnp.broadcast_to(b_row(r0 // (2 * s) * (2 * s) + off), (GLA_SUB, GLA_HK))
                    else:
                        blk = jnp.broadcast_to(b_row(r0 + off), (GLA_SUB, GLA_HK))
                        for t in range(1, GLA_SUB // (2 * s)):
                            blk = jnp.where(sub8 >= 2 * s * t, b_row(r0 + 2 * s * t + off), blk)
                    blocks.append(blk)
                ref = jnp.concatenate(blocks, axis=0)
                e = jnp.exp(-jnp.abs(b - ref))
                level_scores.append(lax.dot_general(
                    (q * e).astype(BF16), (k * e).astype(BF16), NT_DIMS, preferred_element_type=F32))

            k_hat = (k * jnp.exp(b_tot - b)).astype(BF16)
            delta = lax.dot_general(k_hat, v, TN_DIMS, preferred_element_type=F32)
            decay_col = jnp.transpose(jnp.broadcast_to(jnp.exp(b_tot), (GLA_HK, GLA_HK)))
            decay = jnp.concatenate([decay_col] * (GLA_HV // GLA_HK), axis=1)
            staged.append((d, h, o_ref, rows, lv, v, o_inter, diag, level_scores, state * decay + delta))

        for d, h, o_ref, rows, lv, v, o_inter, diag, level_scores, new_state in staged:
            scores = mask_ref[d, len(GLA_LEVELS)] * diag
            for lvl, ls in enumerate(level_scores):
                scores = scores + mask_ref[d, lvl] * ls
            o = o_inter + jnp.dot(scores.astype(BF16), v, preferred_element_type=F32)
            o_ref[0, rows, lv] = o.astype(o_ref.dtype)
            state_ref[d, h] = new_state
        return carry

    lax.fori_loop(0, n_chunks, chunk_body, 0)


def _gla(q, k, v, bcum, *, tile):
    bsz, seq, _ = q.shape
    nt = seq // tile
    masks = jnp.stack([_gla_masks(False), _gla_masks(True)])
    fwd = lambda b, t: (b, t, 0)
    bwd = lambda b, t: (b, nt - 1 - t, 0)
    bwd_b = lambda b, t: (b, nt - 1 - t, 1)
    kernel = functools.partial(_gla_kernel, tile=tile)
    out = jax.ShapeDtypeStruct((bsz, seq, GLA_DV), BF16)
    return pl.pallas_call(
        kernel,
        grid=(bsz, nt),
        in_specs=[
            pl.BlockSpec((1, tile, GLA_DK), fwd), pl.BlockSpec((1, tile, GLA_DK), fwd),
            pl.BlockSpec((1, tile, GLA_DV), fwd), pl.BlockSpec((1, tile, GLA_DK), fwd),
            pl.BlockSpec((1, tile, GLA_DK), bwd), pl.BlockSpec((1, tile, GLA_DK), bwd),
            pl.BlockSpec((1, tile, GLA_DV), bwd), pl.BlockSpec((1, tile, GLA_DK), bwd_b),
            _const_spec(masks.shape),
        ],
        out_specs=[pl.BlockSpec((1, tile, GLA_DV), fwd), pl.BlockSpec((1, tile, GLA_DV), bwd)],
        out_shape=[out, out],
        scratch_shapes=[
            pltpu.VMEM((2, GLA_HEADS, GLA_HK, GLA_HV), F32),
            pltpu.VMEM((2, GLA_HEADS, GLA_CHUNK, GLA_HK), F32),
        ],
        compiler_params=pltpu.CompilerParams(
            dimension_semantics=("parallel", "arbitrary"), vmem_limit_bytes=VMEM_LIMIT_BYTES),
        name="gla",
    )(q, k, v, bcum, q, k, v, bcum, masks)


def _attn_kernel(q_ref, k_ref, vt_ref, lq_ref, lk_ref, gain_ref, o_ref,
                 q2_ref, s_ref, smax_ref, m_ref, acc_ref, *, tq, tk, seq, lam_init):
    n_tiles = seq // tk
    n_q = seq // tq
    kc = min(MXU_DEPTH, tk)
    n_kc = tk // kc
    ones_rows = jnp.ones((ONES_ROWS, kc), BF16)
    lane = lax.broadcasted_iota(jnp.int32, (tq, LANES), 1)

    def stage_queries(qi):
        q = q_ref[0, pl.ds(pl.multiple_of(qi * tq, tq), tq), :]
        zero = jnp.zeros_like(q)
        q2_ref[0:tq] = jnp.where(lane < DIFF_HEAD_DIM, q, zero)
        q2_ref[tq:2 * tq] = jnp.where(lane >= DIFF_HEAD_DIM, q, zero)

    def reset_stats():
        m_ref[...] = jnp.full_like(m_ref, -jnp.inf)
        acc_ref[...] = jnp.zeros_like(acc_ref)

    def scores_chunk(kt, slot, c, smax):
        ks = pl.multiple_of(kt * tk, tk)
        k = k_ref[0, pl.ds(ks + c * kc, kc), :]
        s = lax.dot_general(k, q2_ref[...], NT_DIMS, preferred_element_type=F32)
        s_ref[slot, c * kc:(c + 1) * kc, :] = s
        cmax = jnp.max(s, axis=0, keepdims=True)
        return cmax if smax is None else jnp.maximum(smax, cmax)

    def values_chunk(kt, slot, c, m_new, pv):
        ks = pl.multiple_of(kt * tk, tk)
        vt = jnp.concatenate([vt_ref[0, :, pl.ds(ks + c * kc, kc)], ones_rows], axis=0)
        p = jnp.exp2(s_ref[slot, c * kc:(c + 1) * kc, :] - m_new).astype(BF16)
        d = jnp.dot(vt, p, preferred_element_type=F32)
        return d if pv is None else pv + d

    def step(i, slot, next_kt):
        m_prev = m_ref[...]
        m_new = jnp.maximum(m_prev, smax_ref[slot])
        alpha = jnp.exp2(m_prev - m_new)
        m_ref[...] = m_new
        smax, pv = None, None
        for c in range(n_kc):
            smax = scores_chunk(next_kt, 1 - slot, c, smax)
            pv = values_chunk(i, slot, c, m_new, pv)
        smax_ref[1 - slot] = smax
        acc_ref[...] = alpha * acc_ref[...] + pv

    dots = jnp.sum(lq_ref[...] * lk_ref[...], axis=1, keepdims=True)
    e = jnp.exp(dots)
    lam = e[0:1, :] - e[1:2, :] + lam_init

    def finalize(qi):
        o2 = acc_ref[0:DIFF_HV] / acc_ref[DIFF_HV:DIFF_HV + 1]
        ot = o2[:, 0:tq] - lam * o2[:, tq:2 * tq]
        ms = jnp.mean(ot * ot, axis=0, keepdims=True)
        on = jnp.transpose(ot * lax.rsqrt(ms + EPS))
        o_ref[0, pl.ds(pl.multiple_of(qi * tq, tq), tq), :] = (
            on * gain_ref[...] * (1.0 - lam_init)).astype(o_ref.dtype)

    stage_queries(0)
    reset_stats()
    smax0 = None
    for c in range(n_kc):
        smax0 = scores_chunk(0, 0, c, smax0)
    smax_ref[0] = smax0

    def query_tile(qi, carry):
        def pair(j, c2):
            i = 2 * j
            step(i, 0, i + 1)
            step(i + 1, 1, i + 2)
            return c2

        lax.fori_loop(0, (n_tiles - 2) // 2, pair, 0)
        step(n_tiles - 2, 0, n_tiles - 1)
        stage_queries(jnp.minimum(qi + 1, n_q - 1))
        step(n_tiles - 1, 1, 0)
        finalize(qi)
        reset_stats()
        return carry

    lax.fori_loop(0, n_q, query_tile, 0)


def _attn(dq, dk, dvt, lq, lk, gain, *, lam_init, tq, tk):
    bsz, seq, _ = dq.shape
    assert seq % (2 * tk) == 0 and seq % tq == 0
    kernel = functools.partial(_attn_kernel, tq=tq, tk=tk, seq=seq, lam_init=lam_init)
    head_rows = pl.BlockSpec((1, seq, DIFF_HV), lambda b, h: (b, 0, h))
    return pl.pallas_call(
        kernel,
        grid=(bsz, DIFF_HEADS),
        in_specs=[
            head_rows, head_rows,
            pl.BlockSpec((1, DIFF_HV, seq), lambda b, h: (b, h, 0)),
            pl.BlockSpec((2, DIFF_HEAD_DIM), lambda b, h: (0, 0)),
            pl.BlockSpec((2, DIFF_HEAD_DIM), lambda b, h: (0, 0)),
            pl.BlockSpec((1, DIFF_HV), lambda b, h: (0, 0)),
        ],
        out_specs=head_rows,
        out_shape=jax.ShapeDtypeStruct((bsz, seq, D_MODEL), BF16),
        scratch_shapes=[
            pltpu.VMEM((2 * tq, DIFF_HV), BF16),
            pltpu.VMEM((2, tk, 2 * tq), F32),
            pltpu.VMEM((2, 1, 2 * tq), F32),
            pltpu.VMEM((1, 2 * tq), F32),
            pltpu.VMEM((DIFF_HV + ONES_ROWS, 2 * tq), F32),
        ],
        compiler_params=pltpu.CompilerParams(
            dimension_semantics=("parallel", "parallel"),
            vmem_limit_bytes=VMEM_LIMIT_BYTES),
        name="diff_attn",
    )(dq, dk, dvt, lq, lk, gain)


def _out_kernel(x_ref, gate_ref, of_ref, ob_ref, z_ref, od_ref, dz_ref, mg_ref, md_ref,
                ggain_ref, fgain_ref, wbg_ref, wbd_ref, wo_ref, y_ref):
    og = of_ref[0].astype(F32) + ob_ref[0].astype(F32)
    ggain = ggain_ref[...]
    heads = []
    for h in range(GLA_HEADS):
        oh = og[:, h * GLA_HV:(h + 1) * GLA_HV]
        ms = jnp.mean(oh * oh, axis=1, keepdims=True)
        heads.append(oh * lax.rsqrt(ms + EPS) * ggain)
    og = jnp.concatenate(heads, axis=1) * z_ref[0].astype(F32)
    y_gla = jnp.dot(og.astype(BF16), wbg_ref[...], preferred_element_type=F32)
    od = od_ref[0].astype(F32) * dz_ref[0].astype(F32)
    y_diff = jnp.dot(od.astype(BF16), wbd_ref[...], preferred_element_type=F32)
    merged = mg_ref[0].astype(F32) * y_gla + md_ref[0].astype(F32) * y_diff
    y = jnp.dot(merged.astype(BF16), wo_ref[...], preferred_element_type=F32)
    r = x_ref[0] + gate_ref[0] * y
    ms = jnp.mean(r * r, axis=1, keepdims=True)
    y_ref[0] = r * lax.rsqrt(ms + EPS) * fgain_ref[...]


def _out_stage(x, gate, o_f, o_b, z, o_d, dz, mg, md, ggain, fgain, wbg, wbd, wo, *, tile):
    bsz, seq, _ = x.shape
    row = pl.BlockSpec((1, tile, D_MODEL), lambda b, i: (b, i, 0))
    per_b = pl.BlockSpec((1, 1, D_MODEL), lambda b, i: (b, 0, 0))
    return pl.pallas_call(
        _out_kernel,
        grid=(bsz, seq // tile),
        in_specs=[row, per_b] + [row] * 7 + [
            _const_spec(ggain.shape), _const_spec(fgain.shape),
            _const_spec(wbg.shape), _const_spec(wbd.shape), _const_spec(wo.shape)],
        out_specs=row,
        out_shape=jax.ShapeDtypeStruct((bsz, seq, D_MODEL), F32),
        compiler_params=pltpu.CompilerParams(
            dimension_semantics=("parallel", "parallel"), vmem_limit_bytes=VMEM_LIMIT_BYTES),
        name="out_stage",
    )(x, gate, o_f, o_b, z, o_d, dz, mg, md, ggain, fgain, wbg, wbd, wo)


def _rope_tables(seq):
    half = DIFF_HEAD_DIM // 2
    inv = 1.0 / (ROPE_THETA ** (jnp.arange(half, dtype=F32) / half))
    ang = jnp.arange(seq, dtype=F32)[:, None] * inv[None, :]
    cos = jnp.cos(ang)
    sin = jnp.sin(ang)
    cs = jnp.concatenate([cos, cos, cos, cos], axis=1)
    sn = jnp.concatenate([-sin, sin, -sin, sin], axis=1)
    return cs, sn


def _layer_weights(w_in, w_alpha, b_alpha):
    splits = (GLA_DK, GLA_DK, GLA_DV, GLA_DV, 2 * GLA_RANK) + (D_MODEL,) * 6
    pts = np.cumsum(splits)[:-1].tolist()
    a_q, a_k, a_v, a_z, a_low, d_q, d_k, d_v, d_z, m_g, m_d = jnp.split(w_in, pts, axis=1)
    wg = jnp.concatenate([a_q, a_k, a_v, a_z], axis=1).astype(BF16)
    zeros = jnp.zeros((GLA_RANK, GLA_DK), w_alpha.dtype)
    walpha = jnp.concatenate([
        jnp.concatenate([w_alpha[0], zeros], axis=1),
        jnp.concatenate([zeros, w_alpha[1]], axis=1)], axis=0).astype(BF16)
    balpha = b_alpha.reshape(1, 2 * GLA_DK)
    return (wg, a_low.astype(BF16), walpha, balpha,
            d_q.astype(BF16), d_k.astype(BF16), d_v.T.astype(BF16), d_z.astype(BF16),
            m_g.astype(BF16), m_d.astype(BF16))


def _trunk_layer(x, mod, lam_init, norm_gain, in_weights, gla_norm_gain, lambda_q, lambda_k,
                 diff_norm_gain, wbg, wbd, wo, final_gain, *, row_tile, gla_tile, tq, tk):
    bsz, seq, _ = x.shape
    row_tile, gla_tile, tq, tk = (min(t, s) for t, s in
                                  ((row_tile, seq), (gla_tile, seq), (tq, seq), (tk, seq // 2)))
    shift, scale, gate = [m.reshape(bsz, 1, D_MODEL) for m in jnp.split(mod, 3, axis=-1)]
    cs, sn = _rope_tables(seq)
    (q, k, v, z, bcum, dq, dk, dv, dz, mg, md) = _inproj(
        x, shift, scale, norm_gain.reshape(1, D_MODEL), cs, sn, in_weights, row_tile)
    o_f, o_b = _gla(q, k, v, bcum, tile=gla_tile)
    o_d = _attn(dq, dk, dv, lambda_q, lambda_k, diff_norm_gain.reshape(1, DIFF_HV),
                lam_init=lam_init, tq=tq, tk=tk)
    return _out_stage(x, gate, o_f, o_b, z, o_d, dz, mg, md,
                      gla_norm_gain.reshape(1, GLA_HV), final_gain.reshape(1, D_MODEL),
                      wbg, wbd, wo, tile=row_tile)


def kernel(x_prompt, x_sample, c_prompt, c_sample, w_ada, b_ada, norm_gain, w_in, w_alpha, b_alpha,
           gla_norm_gain, lambda_q, lambda_k, diff_norm_gain, w_bo_gla, w_bo_diff, w_out, final_gain):
    depth = w_in.shape[0]
    assert depth == 1, "final RMSNorm is fused into the (single) layer's output stage"
    layer = 0
    lam_init = 0.8 - 0.6 * math.exp(-0.3 * layer)
    nb_p, nb_s = c_prompt.shape[0], c_sample.shape[0]
    rows = -(-(nb_p + nb_s) // 8) * 8
    c_all = jnp.concatenate(
        [c_prompt, c_sample, jnp.zeros((rows - nb_p - nb_s, D_MODEL), F32)], axis=0)
    mod = _ada(c_all, w_ada[layer].astype(BF16), b_ada[layer].reshape(1, 3 * D_MODEL))
    in_weights = _layer_weights(w_in[layer], w_alpha[layer], b_alpha[layer])
    common = dict(
        lam_init=lam_init, norm_gain=norm_gain[layer], in_weights=in_weights,
        gla_norm_gain=gla_norm_gain[layer], lambda_q=lambda_q[layer], lambda_k=lambda_k[layer],
        diff_norm_gain=diff_norm_gain[layer], wbg=w_bo_gla[layer].astype(BF16),
        wbd=w_bo_diff[layer].astype(BF16), wo=w_out[layer].astype(BF16), final_gain=final_gain)
    y_prompt = _trunk_layer(x_prompt, mod[:nb_p], row_tile=512, gla_tile=512, tq=512, tk=2048, **common)
    y_sample = _trunk_layer(x_sample, mod[nb_p:nb_p + nb_s], row_tile=512, gla_tile=512, tq=512, tk=2048,
                            **common)
    return (y_prompt, y_sample)
```

```python
import functools
import math

import numpy as np
import jax
import jax.numpy as jnp
from jax import lax
from jax.experimental import pallas as pl
from jax.experimental.pallas import tpu as pltpu

D_MODEL = 1024
GLA_HEADS = 4
GLA_DK = D_MODEL // 2
GLA_DV = D_MODEL
GLA_HK = GLA_DK // GLA_HEADS
GLA_HV = GLA_DV // GLA_HEADS
GLA_RANK = 16
GLA_LOGIT_NORM = 16.0
GLA_CHUNK = 64
GLA_SUB = 8
GLA_LEVELS = (1, 2, 4, 8, 16, 32)
DIFF_HEAD_DIM = 64
DIFF_HEADS = D_MODEL // (2 * DIFF_HEAD_DIM)
DIFF_HV = 2 * DIFF_HEAD_DIM
ROPE_THETA = 10000.0
EPS = 1e-6
LANES = 128
ONES_ROWS = 16
MXU_DEPTH = 256
ATTN_Q_SCALE = (DIFF_HEAD_DIM ** -0.5) * math.log2(math.e)

F32 = jnp.float32
BF16 = jnp.bfloat16

VMEM_LIMIT_BYTES = 56 * 1024 * 1024

NT_DIMS = (((1,), (1,)), ((), ()))
TN_DIMS = (((0,), (0,)), ((), ()))


def _sigmoid(x):
    return 1.0 / (1.0 + jnp.exp(-x))


def _const_spec(shape):
    nd = len(shape)
    return pl.BlockSpec(shape, lambda *_: (0,) * nd, pipeline_mode=pl.Buffered(1))


def _ada_kernel(c_ref, w_ref, b_ref, o_ref):
    c = c_ref[...]
    s = c * _sigmoid(c)
    o_ref[...] = jnp.dot(s.astype(BF16), w_ref[...], preferred_element_type=F32) + b_ref[...]


def _ada(c, w_bf16, b):
    rows = c.shape[0]
    n_out = w_bf16.shape[1]
    tn = D_MODEL
    return pl.pallas_call(
        _ada_kernel,
        grid=(n_out // tn,),
        in_specs=[
            pl.BlockSpec((rows, D_MODEL), lambda j: (0, 0)),
            pl.BlockSpec((D_MODEL, tn), lambda j: (0, j)),
            pl.BlockSpec((1, tn), lambda j: (0, j)),
        ],
        out_specs=pl.BlockSpec((rows, tn), lambda j: (0, j)),
        out_shape=jax.ShapeDtypeStruct((rows, n_out), F32),
        name="ada",
    )(c, w_bf16, b)


def _rope_slices(x, cs, sn):
    lane = lax.broadcasted_iota(jnp.int32, (x.shape[0], LANES), 1)
    is_x2 = (lane & 32) != 0
    outs = []
    for s in range(x.shape[1] // LANES):
        xs = x[:, s * LANES:(s + 1) * LANES]
        from_lo = pltpu.roll(xs, 32, axis=1)
        from_hi = pltpu.roll(xs, LANES - 32, axis=1)
        partner = jnp.where(is_x2, from_lo, from_hi)
        outs.append(xs * cs + partner * sn)
    return jnp.concatenate(outs, axis=1)


def _inproj_kernel(x_ref, shift_ref, scale_ref, gain_ref, cs_ref, sn_ref, trif_ref, trib_ref,
                   wg_ref, wlow_ref, walpha_ref, balpha_ref,
                   wdq_ref, wdk_ref, wdv_ref, wdz_ref, wmg_ref, wmd_ref,
                   q_out, k_out, v_out, z_out, b_out,
                   dq_out, dk_out, dv_out, dz_out, mg_out, md_out):
    x = x_ref[0]
    ms = jnp.mean(x * x, axis=-1, keepdims=True)
    y = x * lax.rsqrt(ms + EPS) * gain_ref[...]
    h = (y * (1.0 + scale_ref[0]) + shift_ref[0]).astype(BF16)

    pg = jnp.dot(h, wg_ref[...], preferred_element_type=F32)
    q_out[0] = (pg[:, 0:GLA_DK] * (GLA_HK ** -0.5)).astype(BF16)
    k_out[0] = pg[:, GLA_DK:2 * GLA_DK].astype(BF16)
    v_out[0] = pg[:, 2 * GLA_DK:2 * GLA_DK + GLA_DV].astype(BF16)
    az = pg[:, 2 * GLA_DK + GLA_DV:]
    z_out[0] = (az * _sigmoid(az)).astype(BF16)

    low = jnp.dot(h, wlow_ref[...], preferred_element_type=F32)
    logits = jnp.dot(low.astype(BF16), walpha_ref[...], preferred_element_type=F32) + balpha_ref[...]
    g = (jnp.minimum(logits, 0.0) - jnp.log1p(jnp.exp(-jnp.abs(logits)))) * (1.0 / GLA_LOGIT_NORM)
    g_hi = g.astype(BF16)
    g_lo = (g - g_hi.astype(F32)).astype(BF16)
    tr = trif_ref.shape[0]
    for tri_ref, lo in ((trif_ref, 0), (trib_ref, GLA_DK)):
        tri = tri_ref[...]
        for r in range(0, g.shape[0], tr):
            b_out[0, r:r + tr, lo:lo + GLA_DK] = (
                jnp.dot(tri, g_hi[r:r + tr, lo:lo + GLA_DK], preferred_element_type=F32)
                + jnp.dot(tri, g_lo[r:r + tr, lo:lo + GLA_DK], preferred_element_type=F32))

    cs = cs_ref[...]
    sn = sn_ref[...]
    dq = jnp.dot(h, wdq_ref[...], preferred_element_type=F32)
    dq_out[0] = (_rope_slices(dq, cs, sn) * ATTN_Q_SCALE).astype(BF16)
    dk = jnp.dot(h, wdk_ref[...], preferred_element_type=F32)
    dk_out[0] = _rope_slices(dk, cs, sn).astype(BF16)
    dv_out[0] = lax.dot_general(wdv_ref[...], h, NT_DIMS, preferred_element_type=F32).astype(BF16)
    dz = jnp.dot(h, wdz_ref[...], preferred_element_type=F32)
    dz_out[0] = (dz * _sigmoid(dz)).astype(BF16)
    mg_out[0] = _sigmoid(jnp.dot(h, wmg_ref[...], preferred_element_type=F32)).astype(BF16)
    md_out[0] = _sigmoid(jnp.dot(h, wmd_ref[...], preferred_element_type=F32)).astype(BF16)


def _chunk_tri(tile, reverse):
    i = np.arange(tile)[:, None]
    j = np.arange(tile)[None, :]
    same = (i // GLA_CHUNK) == (j // GLA_CHUNK)
    tri = (j >= i) if reverse else (j <= i)
    return jnp.asarray(same & tri, dtype=BF16)


def _inproj(x, shift, scale, gain, cs, sn, weights, tile):
    bsz, seq, _ = x.shape
    (wg, wlow, walpha, balpha, wdq, wdk, wdv, wdz, wmg, wmd) = weights
    tri_rows = min(tile, MXU_DEPTH)
    trif = _chunk_tri(tri_rows, False)
    trib = _chunk_tri(tri_rows, True)
    row = lambda c: pl.BlockSpec((1, tile, c), lambda b, i: (b, i, 0))
    per_b = pl.BlockSpec((1, 1, D_MODEL), lambda b, i: (b, 0, 0))
    consts = [gain, None, None, trif, trib, wg, wlow, walpha, balpha, wdq, wdk, wdv, wdz, wmg, wmd]
    in_specs = [row(D_MODEL), per_b, per_b]
    for a in consts:
        if a is None:
            in_specs.append(pl.BlockSpec((tile, LANES), lambda b, i: (i, 0)))
        else:
            in_specs.append(_const_spec(a.shape))
    out_cols = [(GLA_DK, BF16), (GLA_DK, BF16), (GLA_DV, BF16), (GLA_DV, BF16), (2 * GLA_DK, F32)] + \
               [(D_MODEL, BF16)] * 6
    out_specs = [row(c) for c, _ in out_cols]
    out_shape = [jax.ShapeDtypeStruct((bsz, seq, c), dt) for c, dt in out_cols]
    dv_index = 7
    out_specs[dv_index] = pl.BlockSpec((1, D_MODEL, tile), lambda b, i: (b, 0, i))
    out_shape[dv_index] = jax.ShapeDtypeStruct((bsz, D_MODEL, seq), BF16)
    return pl.pallas_call(
        _inproj_kernel,
        grid=(bsz, seq // tile),
        in_specs=in_specs,
        out_specs=out_specs,
        out_shape=out_shape,
        compiler_params=pltpu.CompilerParams(
            dimension_semantics=("parallel", "parallel"), vmem_limit_bytes=VMEM_LIMIT_BYTES),
        name="inproj",
    )(x, shift, scale, gain, cs, sn, trif, trib, wg, wlow, walpha, balpha, wdq, wdk, wdv, wdz, wmg, wmd)


def _gla_masks(reverse):
    c = GLA_CHUNK
    i = np.arange(c)[:, None]
    j = np.arange(c)[None, :]
    q_half, k_half = (0, 1) if reverse else (1, 0)
    masks = []
    for s in GLA_LEVELS:
        same_parent = (i // (2 * s)) == (j // (2 * s))
        masks.append(same_parent & ((i // s) % 2 == q_half) & ((j // s) % 2 == k_half))
    masks.append(i == j)
    return jnp.asarray(np.stack(masks).astype(np.float32))


def _gla_kernel(qf_ref, kf_ref, vf_ref, bf_ref, qb_ref, kb_ref, vb_ref, bb_ref, mask_ref,
                of_ref, ob_ref, state_ref, b32_ref, *, tile):
    c = GLA_CHUNK
    n_chunks = tile // c
    refs = {False: (qf_ref, kf_ref, vf_ref, bf_ref, of_ref), True: (qb_ref, kb_ref, vb_ref, bb_ref, ob_ref)}
    chains = [(reverse, h) for reverse in (False, True) for h in range(GLA_HEADS)]

    @pl.when(pl.program_id(1) == 0)
    def _():
        state_ref[...] = jnp.zeros_like(state_ref)

    sub8 = lax.broadcasted_iota(jnp.int32, (GLA_SUB, GLA_HK), 0)

    def chunk_body(ci, carry):
        row0 = {False: pl.multiple_of(ci * c, c), True: pl.multiple_of((n_chunks - 1 - ci) * c, c)}
        staged = []
        for reverse, h in chains:
            q_ref, k_ref, v_ref, b_ref, o_ref = refs[reverse]
            d = int(reverse)
            rows = pl.ds(row0[reverse], c)
            scan_last = 0 if reverse else c - 1
            lk = slice(h * GLA_HK, (h + 1) * GLA_HK)
            lv = slice(h * GLA_HV, (h + 1) * GLA_HV)
            q = q_ref[0, rows, lk].astype(F32)
            k = k_ref[0, rows, lk].astype(F32)
            v = v_ref[0, rows, lv]
            b = b_ref[0, rows, lk]
            b32_ref[d, h] = b

            def b_row(r, d=d, h=h):
                return b32_ref[d, h, pl.ds(r, 1), :]

            b_tot = b_row(scan_last)

            state = state_ref[d, h]
            q_in = (q * jnp.exp(b)).astype(BF16)
            o_inter = jnp.dot(q_in, state.astype(BF16), preferred_element_type=F32)

            diag = jnp.sum(q * k, axis=1, keepdims=True)
            level_scores = []
            for s in GLA_LEVELS:
                off = s if reverse else s - 1
                blocks = []
                for jb in range(c // GLA_SUB):
                    r0 = jb * GLA_SUB
                    if 2 * s >= GLA_SUB:
                        blk = jnp.broadcast_to(b_row(r0 // (2 * s) * (2 * s) + off), (GLA_SUB, GLA_HK))
                    else:
                        blk = jnp.broadcast_to(b_row(r0 + off), (GLA_SUB, GLA_HK))
                        for t in range(1, GLA_SUB // (2 * s)):
                            blk = jnp.where(sub8 >= 2 * s * t, b_row(r0 + 2 * s * t + off), blk)
                    blocks.append(blk)
                ref = jnp.concatenate(blocks, axis=0)
                e = jnp.exp(-jnp.abs(b - ref))
                level_scores.append(lax.dot_general(
                    (q * e).astype(BF16), (k * e).astype(BF16), NT_DIMS, preferred_element_type=F32))

            k_hat = (k * jnp.exp(b_tot - b)).astype(BF16)
            delta = lax.dot_general(k_hat, v, TN_DIMS, preferred_element_type=F32)
            decay_col = jnp.transpose(jnp.broadcast_to(jnp.exp(b_tot), (GLA_HK, GLA_HK)))
            decay = jnp.concatenate([decay_col] * (GLA_HV // GLA_HK), axis=1)
            staged.append((d, h, o_ref, rows, lv, v, o_inter, diag, level_scores, state * decay + delta))

        for d, h, o_ref, rows, lv, v, o_inter, diag, level_scores, new_state in staged:
            scores = mask_ref[d, len(GLA_LEVELS)] * diag
            for lvl, ls in enumerate(level_scores):
                scores = scores + mask_ref[d, lvl] * ls
            o = o_inter + jnp.dot(scores.astype(BF16), v, preferred_element_type=F32)
            o_ref[0, rows, lv] = o.astype(o_ref.dtype)
            state_ref[d, h] = new_state
        return carry

    lax.fori_loop(0, n_chunks, chunk_body, 0)


def _gla(q, k, v, bcum, *, tile):
    bsz, seq, _ = q.shape
    nt = seq // tile
    masks = jnp.stack([_gla_masks(False), _gla_masks(True)])
    fwd = lambda b, t: (b, t, 0)
    bwd = lambda b, t: (b, nt - 1 - t, 0)
    bwd_b = lambda b, t: (b, nt - 1 - t, 1)
    kernel = functools.partial(_gla_kernel, tile=tile)
    out = jax.ShapeDtypeStruct((bsz, seq, GLA_DV), BF16)
    return pl.pallas_call(
        kernel,
        grid=(bsz, nt),
        in_specs=[
            pl.BlockSpec((1, tile, GLA_DK), fwd), pl.BlockSpec((1, tile, GLA_DK), fwd),
            pl.BlockSpec((1, tile, GLA_DV), fwd), pl.BlockSpec((1, tile, GLA_DK), fwd),
            pl.BlockSpec((1, tile, GLA_DK), bwd), pl.BlockSpec((1, tile, GLA_DK), bwd),
            pl.BlockSpec((1, tile, GLA_DV), bwd), pl.BlockSpec((1, tile, GLA_DK), bwd_b),
            _const_spec(masks.shape),
        ],
        out_specs=[pl.BlockSpec((1, tile, GLA_DV), fwd), pl.BlockSpec((1, tile, GLA_DV), bwd)],
        out_shape=[out, out],
        scratch_shapes=[
            pltpu.VMEM((2, GLA_HEADS, GLA_HK, GLA_HV), F32),
            pltpu.VMEM((2, GLA_HEADS, GLA_CHUNK, GLA_HK), F32),
        ],
        compiler_params=pltpu.CompilerParams(
            dimension_semantics=("parallel", "arbitrary"), vmem_limit_bytes=VMEM_LIMIT_BYTES),
        name="gla",
    )(q, k, v, bcum, q, k, v, bcum, masks)


def _attn_kernel(q_ref, k_ref, vt_ref, lq_ref, lk_ref, gain_ref, o_ref,
                 q2_ref, s_ref, smax_ref, m_ref, acc_ref, *, tq, tk, seq, lam_init):
    n_tiles = seq // tk
    n_q = seq // tq
    kc = min(MXU_DEPTH, tk)
    n_kc = tk // kc
    ones_rows = jnp.ones((ONES_ROWS, kc), BF16)
    lane = lax.broadcasted_iota(jnp.int32, (tq, LANES), 1)

    def stage_queries(qi):
        q = q_ref[0, pl.ds(pl.multiple_of(qi * tq, tq), tq), :]
        zero = jnp.zeros_like(q)
        q2_ref[0:tq] = jnp.where(lane < DIFF_HEAD_DIM, q, zero)
        q2_ref[tq:2 * tq] = jnp.where(lane >= DIFF_HEAD_DIM, q, zero)

    def reset_stats():
        m_ref[...] = jnp.full_like(m_ref, -jnp.inf)
        acc_ref[...] = jnp.zeros_like(acc_ref)

    def scores_chunk(kt, slot, c, smax):
        ks = pl.multiple_of(kt * tk, tk)
        k = k_ref[0, pl.ds(ks + c * kc, kc), :]
        s = lax.dot_general(k, q2_ref[...], NT_DIMS, preferred_element_type=F32)
        s_ref[slot, c * kc:(c + 1) * kc, :] = s
        cmax = jnp.max(s, axis=0, keepdims=True)
        return cmax if smax is None else jnp.maximum(smax, cmax)

    def values_chunk(kt, slot, c, m_new, pv):
        ks = pl.multiple_of(kt * tk, tk)
        vt = jnp.concatenate([vt_ref[0, :, pl.ds(ks + c * kc, kc)], ones_rows], axis=0)
        p = jnp.exp2(s_ref[slot, c * kc:(c + 1) * kc, :] - m_new).astype(BF16)
        d = jnp.dot(vt, p, preferred_element_type=F32)
        return d if pv is None else pv + d

    def step(i, slot, next_kt):
        m_prev = m_ref[...]
        m_new = jnp.maximum(m_prev, smax_ref[slot])
        alpha = jnp.exp2(m_prev - m_new)
        m_ref[...] = m_new
        smax, pv = None, None
        for c in range(n_kc):
            smax = scores_chunk(next_kt, 1 - slot, c, smax)
            pv = values_chunk(i, slot, c, m_new, pv)
        smax_ref[1 - slot] = smax
        acc_ref[...] = alpha * acc_ref[...] + pv

    dots = jnp.sum(lq_ref[...] * lk_ref[...], axis=1, keepdims=True)
    e = jnp.exp(dots)
    lam = e[0:1, :] - e[1:2, :] + lam_init

    def finalize(qi):
        o2 = acc_ref[0:DIFF_HV] / acc_ref[DIFF_HV:DIFF_HV + 1]
        ot = o2[:, 0:tq] - lam * o2[:, tq:2 * tq]
        ms = jnp.mean(ot * ot, axis=0, keepdims=True)
        on = jnp.transpose(ot * lax.rsqrt(ms + EPS))
        o_ref[0, pl.ds(pl.multiple_of(qi * tq, tq), tq), :] = (
            on * gain_ref[...] * (1.0 - lam_init)).astype(o_ref.dtype)

    stage_queries(0)
    reset_stats()
    smax0 = None
    for c in range(n_kc):
        smax0 = scores_chunk(0, 0, c, smax0)
    smax_ref[0] = smax0

    def query_tile(qi, carry):
        def pair(j, c2):
            i = 2 * j
            step(i, 0, i + 1)
            step(i + 1, 1, i + 2)
            return c2

        lax.fori_loop(0, (n_tiles - 2) // 2, pair, 0)
        step(n_tiles - 2, 0, n_tiles - 1)
        stage_queries(jnp.minimum(qi + 1, n_q - 1))
        step(n_tiles - 1, 1, 0)
        finalize(qi)
        reset_stats()
        return carry

    lax.fori_loop(0, n_q, query_tile, 0)


def _attn(dq, dk, dvt, lq, lk, gain, *, lam_init, tq, tk):
    bsz, seq, _ = dq.shape
    assert seq % (2 * tk) == 0 and seq % tq == 0
    kernel = functools.partial(_attn_kernel, tq=tq, tk=tk, seq=seq, lam_init=lam_init)
    head_rows = pl.BlockSpec((1, seq, DIFF_HV), lambda b, h: (b, 0, h))
    return pl.pallas_call(
        kernel,
        grid=(bsz, DIFF_HEADS),
        in_specs=[
            head_rows, head_rows,
            pl.BlockSpec((1, DIFF_HV, seq), lambda b, h: (b, h, 0)),
            pl.BlockSpec((2, DIFF_HEAD_DIM), lambda b, h: (0, 0)),
            pl.BlockSpec((2, DIFF_HEAD_DIM), lambda b, h: (0, 0)),
            pl.BlockSpec((1, DIFF_HV), lambda b, h: (0, 0)),
        ],
        out_specs=head_rows,
        out_shape=jax.ShapeDtypeStruct((bsz, seq, D_MODEL), BF16),
        scratch_shapes=[
            pltpu.VMEM((2 * tq, DIFF_HV), BF16),
            pltpu.VMEM((2, tk, 2 * tq), F32),
            pltpu.VMEM((2, 1, 2 * tq), F32),
            pltpu.VMEM((1, 2 * tq), F32),
            pltpu.VMEM((DIFF_HV + ONES_ROWS, 2 * tq), F32),
        ],
        compiler_params=pltpu.CompilerParams(
            dimension_semantics=("parallel", "parallel"),
            vmem_limit_bytes=VMEM_LIMIT_BYTES),
        name="diff_attn",
    )(dq, dk, dvt, lq, lk, gain)


def _out_kernel(x_ref, gate_ref, of_ref, ob_ref, z_ref, od_ref, dz_ref, mg_ref, md_ref,
                ggain_ref, fgain_ref, wbg_ref, wbd_ref, wo_ref, y_ref):
    og = of_ref[0].astype(F32) + ob_ref[0].astype(F32)
    ggain = ggain_ref[...]
    heads = []
    for h in range(GLA_HEADS):
        oh = og[:, h * GLA_HV:(h + 1) * GLA_HV]
        ms = jnp.mean(oh * oh, axis=1, keepdims=True)
        heads.append(oh * lax.rsqrt(ms + EPS) * ggain)
    og = jnp.concatenate(heads, axis=1) * z_ref[0].astype(F32)
    y_gla = jnp.dot(og.astype(BF16), wbg_ref[...], preferred_element_type=F32)
    od = od_ref[0].astype(F32) * dz_ref[0].astype(F32)
    y_diff = jnp.dot(od.astype(BF16), wbd_ref[...], preferred_element_type=F32)
    merged = mg_ref[0].astype(F32) * y_gla + md_ref[0].astype(F32) * y_diff
    y = jnp.dot(merged.astype(BF16), wo_ref[...], preferred_element_type=F32)
    r = x_ref[0] + gate_ref[0] * y
    ms = jnp.mean(r * r, axis=1, keepdims=True)
    y_ref[0] = r * lax.rsqrt(ms + EPS) * fgain_ref[...]


def _out_stage(x, gate, o_f, o_b, z, o_d, dz, mg, md, ggain, fgain, wbg, wbd, wo, *, tile):
    bsz, seq, _ = x.shape
    row = pl.BlockSpec((1, tile, D_MODEL), lambda b, i: (b, i, 0))
    per_b = pl.BlockSpec((1, 1, D_MODEL), lambda b, i: (b, 0, 0))
    return pl.pallas_call(
        _out_kernel,
        grid=(bsz, seq // tile),
        in_specs=[row, per_b] + [row] * 7 + [
            _const_spec(ggain.shape), _const_spec(fgain.shape),
            _const_spec(wbg.shape), _const_spec(wbd.shape), _const_spec(wo.shape)],
        out_specs=row,
        out_shape=jax.ShapeDtypeStruct((bsz, seq, D_MODEL), F32),
        compiler_params=pltpu.CompilerParams(
            dimension_semantics=("parallel", "parallel"), vmem_limit_bytes=VMEM_LIMIT_BYTES),
        name="out_stage",
    )(x, gate, o_f, o_b, z, o_d, dz, mg, md, ggain, fgain, wbg, wbd, wo)


def _rope_tables(seq):
    half = DIFF_HEAD_DIM // 2
    inv = 1.0 / (ROPE_THETA ** (jnp.arange(half, dtype=F32) / half))
    ang = jnp.arange(seq, dtype=F32)[:, None] * inv[None, :]
    cos = jnp.cos(ang)
    sin = jnp.sin(ang)
    cs = jnp.concatenate([cos, cos, cos, cos], axis=1)
    sn = jnp.concatenate([-sin, sin, -sin, sin], axis=1)
    return cs, sn


def _layer_weights(w_in, w_alpha, b_alpha):
    splits = (GLA_DK, GLA_DK, GLA_DV, GLA_DV, 2 * GLA_RANK) + (D_MODEL,) * 6
    pts = np.cumsum(splits)[:-1].tolist()
    a_q, a_k, a_v, a_z, a_low, d_q, d_k, d_v, d_z, m_g, m_d = jnp.split(w_in, pts, axis=1)
    wg = jnp.concatenate([a_q, a_k, a_v, a_z], axis=1).astype(BF16)
    zeros = jnp.zeros((GLA_RANK, GLA_DK), w_alpha.dtype)
    walpha = jnp.concatenate([
        jnp.concatenate([w_alpha[0], zeros], axis=1),
        jnp.concatenate([zeros, w_alpha[1]], axis=1)], axis=0).astype(BF16)
    balpha = b_alpha.reshape(1, 2 * GLA_DK)
    return (wg, a_low.astype(BF16), walpha, balpha,
            d_q.astype(BF16), d_k.astype(BF16), d_v.T.astype(BF16), d_z.astype(BF16),
            m_g.astype(BF16), m_d.astype(BF16))


def _trunk_layer(x, mod, lam_init, norm_gain, in_weights, gla_norm_gain, lambda_q, lambda_k,
                 diff_norm_gain, wbg, wbd, wo, final_gain, *, row_tile, gla_tile, tq, tk):
    bsz, seq, _ = x.shape
    row_tile, gla_tile, tq, tk = (min(t, s) for t, s in
                                  ((row_tile, seq), (gla_tile, seq), (tq, seq), (tk, seq // 2)))
    shift, scale, gate = [m.reshape(bsz, 1, D_MODEL) for m in jnp.split(mod, 3, axis=-1)]
    cs, sn = _rope_tables(seq)
    (q, k, v, z, bcum, dq, dk, dv, dz, mg, md) = _inproj(
        x, shift, scale, norm_gain.reshape(1, D_MODEL), cs, sn, in_weights, row_tile)
    o_f, o_b = _gla(q, k, v, bcum, tile=gla_tile)
    o_d = _attn(dq, dk, dv, lambda_q, lambda_k, diff_norm_gain.reshape(1, DIFF_HV),
                lam_init=lam_init, tq=tq, tk=tk)
    return _out_stage(x, gate, o_f, o_b, z, o_d, dz, mg, md,
                      gla_norm_gain.reshape(1, GLA_HV), final_gain.reshape(1, D_MODEL),
                      wbg, wbd, wo, tile=row_tile)


def kernel(x_prompt, x_sample, c_prompt, c_sample, w_ada, b_ada, norm_gain, w_in, w_alpha, b_alpha,
           gla_norm_gain, lambda_q, lambda_k, diff_norm_gain, w_bo_gla, w_bo_diff, w_out, final_gain):
    depth = w_in.shape[0]
    assert depth == 1, "final RMSNorm is fused into the (single) layer's output stage"
    layer = 0
    lam_init = 0.8 - 0.6 * math.exp(-0.3 * layer)
    nb_p, nb_s = c_prompt.shape[0], c_sample.shape[0]
    rows = -(-(nb_p + nb_s) // 8) * 8
    c_all = jnp.concatenate(
        [c_prompt, c_sample, jnp.zeros((rows - nb_p - nb_s, D_MODEL), F32)], axis=0)
    mod = _ada(c_all, w_ada[layer].astype(BF16), b_ada[layer].reshape(1, 3 * D_MODEL))
    in_weights = _layer_weights(w_in[layer], w_alpha[layer], b_alpha[layer])
    common = dict(
        lam_init=lam_init, norm_gain=norm_gain[layer], in_weights=in_weights,
        gla_norm_gain=gla_norm_gain[layer], lambda_q=lambda_q[layer], lambda_k=lambda_k[layer],
        diff_norm_gain=diff_norm_gain[layer], wbg=w_bo_gla[layer].astype(BF16),
        wbd=w_bo_diff[layer].astype(BF16), wo=w_out[layer].astype(BF16), final_gain=final_gain)
    y_prompt = _trunk_layer(x_prompt, mod[:nb_p], row_tile=512, gla_tile=512, tq=256, tk=4096, **common)
    y_sample = _trunk_layer(x_sample, mod[nb_p:nb_p + nb_s], row_tile=512, gla_tile=512, tq=512, tk=2048,
                            **common)
    return (y_prompt, y_sample)
```

```python
import functools
import math

import numpy as np
import jax
import jax.numpy as jnp
from jax import lax
from jax.experimental import pallas as pl
from jax.experimental.pallas import tpu as pltpu

D_MODEL = 1024
GLA_HEADS = 4
GLA_DK = D_MODEL // 2
GLA_DV = D_MODEL
GLA_HK = GLA_DK // GLA_HEADS
GLA_HV = GLA_DV // GLA_HEADS
GLA_RANK = 16
GLA_LOGIT_NORM = 16.0
GLA_CHUNK = 64
GLA_SUB = 8
GLA_LEVELS = (1, 2, 4, 8, 16, 32)
DIFF_HEAD_DIM = 64
DIFF_HEADS = D_MODEL // (2 * DIFF_HEAD_DIM)
DIFF_HV = 2 * DIFF_HEAD_DIM
ROPE_THETA = 10000.0
EPS = 1e-6
LANES = 128
ONES_ROWS = 16
MXU_DEPTH = 256
ATTN_Q_SCALE = (DIFF_HEAD_DIM ** -0.5) * math.log2(math.e)

F32 = jnp.float32
BF16 = jnp.bfloat16

VMEM_LIMIT_BYTES = 56 * 1024 * 1024

NT_DIMS = (((1,), (1,)), ((), ()))
TN_DIMS = (((0,), (0,)), ((), ()))


def _sigmoid(x):
    return 1.0 / (1.0 + jnp.exp(-x))


def _const_spec(shape):
    nd = len(shape)
    return pl.BlockSpec(shape, lambda *_: (0,) * nd, pipeline_mode=pl.Buffered(1))


def _ada_kernel(c_ref, w_ref, b_ref, o_ref):
    c = c_ref[...]
    s = c * _sigmoid(c)
    o_ref[...] = jnp.dot(s.astype(BF16), w_ref[...], preferred_element_type=F32) + b_ref[...]


def _ada(c, w_bf16, b):
    rows = c.shape[0]
    n_out = w_bf16.shape[1]
    tn = D_MODEL
    return pl.pallas_call(
        _ada_kernel,
        grid=(n_out // tn,),
        in_specs=[
            pl.BlockSpec((rows, D_MODEL), lambda j: (0, 0)),
            pl.BlockSpec((D_MODEL, tn), lambda j: (0, j)),
            pl.BlockSpec((1, tn), lambda j: (0, j)),
        ],
        out_specs=pl.BlockSpec((rows, tn), lambda j: (0, j)),
        out_shape=jax.ShapeDtypeStruct((rows, n_out), F32),
        name="ada",
    )(c, w_bf16, b)


def _rope_slices(x, cs, sn):
    lane = lax.broadcasted_iota(jnp.int32, (x.shape[0], LANES), 1)
    is_x2 = (lane & 32) != 0
    outs = []
    for s in range(x.shape[1] // LANES):
        xs = x[:, s * LANES:(s + 1) * LANES]
        from_lo = pltpu.roll(xs, 32, axis=1)
        from_hi = pltpu.roll(xs, LANES - 32, axis=1)
        partner = jnp.where(is_x2, from_lo, from_hi)
        outs.append(xs * cs + partner * sn)
    return jnp.concatenate(outs, axis=1)


def _inproj_kernel(x_ref, shift_ref, scale_ref, gain_ref, cs_ref, sn_ref, trif_ref, trib_ref,
                   wg_ref, wlow_ref, walpha_ref, balpha_ref,
                   wdq_ref, wdk_ref, wdv_ref, wdz_ref, wmg_ref, wmd_ref,
                   q_out, k_out, v_out, z_out, b_out,
                   dq_out, dk_out, dv_out, dz_out, mg_out, md_out):
    x = x_ref[0]
    ms = jnp.mean(x * x, axis=-1, keepdims=True)
    y = x * lax.rsqrt(ms + EPS) * gain_ref[...]
    h = (y * (1.0 + scale_ref[0]) + shift_ref[0]).astype(BF16)

    pg = jnp.dot(h, wg_ref[...], preferred_element_type=F32)
    q_out[0] = (pg[:, 0:GLA_DK] * (GLA_HK ** -0.5)).astype(BF16)
    k_out[0] = pg[:, GLA_DK:2 * GLA_DK].astype(BF16)
    v_out[0] = pg[:, 2 * GLA_DK:2 * GLA_DK + GLA_DV].astype(BF16)
    az = pg[:, 2 * GLA_DK + GLA_DV:]
    z_out[0] = (az * _sigmoid(az)).astype(BF16)

    low = jnp.dot(h, wlow_ref[...], preferred_element_type=F32)
    logits = jnp.dot(low.astype(BF16), walpha_ref[...], preferred_element_type=F32) + balpha_ref[...]
    g = (jnp.minimum(logits, 0.0) - jnp.log1p(jnp.exp(-jnp.abs(logits)))) * (1.0 / GLA_LOGIT_NORM)
    g_hi = g.astype(BF16)
    g_lo = (g - g_hi.astype(F32)).astype(BF16)
    tr = trif_ref.shape[0]
    for tri_ref, lo in ((trif_ref, 0), (trib_ref, GLA_DK)):
        tri = tri_ref[...]
        for r in range(0, g.shape[0], tr):
            b_out[0, r:r + tr, lo:lo + GLA_DK] = (
                jnp.dot(tri, g_hi[r:r + tr, lo:lo + GLA_DK], preferred_element_type=F32)
                + jnp.dot(tri, g_lo[r:r + tr, lo:lo + GLA_DK], preferred_element_type=F32))

    cs = cs_ref[...]
    sn = sn_ref[...]
    dq = jnp.dot(h, wdq_ref[...], preferred_element_type=F32)
    dq_out[0] = (_rope_slices(dq, cs, sn) * ATTN_Q_SCALE).astype(BF16)
    dk = jnp.dot(h, wdk_ref[...], preferred_element_type=F32)
    dk_out[0] = _rope_slices(dk, cs, sn).astype(BF16)
    dv_out[0] = lax.dot_general(wdv_ref[...], h, NT_DIMS, preferred_element_type=F32).astype(BF16)
    dz = jnp.dot(h, wdz_ref[...], preferred_element_type=F32)
    dz_out[0] = (dz * _sigmoid(dz)).astype(BF16)
    mg_out[0] = _sigmoid(jnp.dot(h, wmg_ref[...], preferred_element_type=F32)).astype(BF16)
    md_out[0] = _sigmoid(jnp.dot(h, wmd_ref[...], preferred_element_type=F32)).astype(BF16)


def _chunk_tri(tile, reverse):
    i = np.arange(tile)[:, None]
    j = np.arange(tile)[None, :]
    same = (i // GLA_CHUNK) == (j // GLA_CHUNK)
    tri = (j >= i) if reverse else (j <= i)
    return jnp.asarray(same & tri, dtype=BF16)


def _inproj(x, shift, scale, gain, cs, sn, weights, tile):
    bsz, seq, _ = x.shape
    (wg, wlow, walpha, balpha, wdq, wdk, wdv, wdz, wmg, wmd) = weights
    tri_rows = min(tile, MXU_DEPTH)
    trif = _chunk_tri(tri_rows, False)
    trib = _chunk_tri(tri_rows, True)
    row = lambda c: pl.BlockSpec((1, tile, c), lambda b, i: (b, i, 0))
    per_b = pl.BlockSpec((1, 1, D_MODEL), lambda b, i: (b, 0, 0))
    consts = [gain, None, None, trif, trib, wg, wlow, walpha, balpha, wdq, wdk, wdv, wdz, wmg, wmd]
    in_specs = [row(D_MODEL), per_b, per_b]
    for a in consts:
        if a is None:
            in_specs.append(pl.BlockSpec((tile, LANES), lambda b, i: (i, 0)))
        else:
            in_specs.append(_const_spec(a.shape))
    out_cols = [(GLA_DK, BF16), (GLA_DK, BF16), (GLA_DV, BF16), (GLA_DV, BF16), (2 * GLA_DK, F32)] + \
               [(D_MODEL, BF16)] * 6
    out_specs = [row(c) for c, _ in out_cols]
    out_shape = [jax.ShapeDtypeStruct((bsz, seq, c), dt) for c, dt in out_cols]
    dv_index = 7
    out_specs[dv_index] = pl.BlockSpec((1, D_MODEL, tile), lambda b, i: (b, 0, i))
    out_shape[dv_index] = jax.ShapeDtypeStruct((bsz, D_MODEL, seq), BF16)
    return pl.pallas_call(
        _inproj_kernel,
        grid=(bsz, seq // tile),
        in_specs=in_specs,
        out_specs=out_specs,
        out_shape=out_shape,
        compiler_params=pltpu.CompilerParams(
            dimension_semantics=("parallel", "parallel"), vmem_limit_bytes=VMEM_LIMIT_BYTES),
        name="inproj",
    )(x, shift, scale, gain, cs, sn, trif, trib, wg, wlow, walpha, balpha, wdq, wdk, wdv, wdz, wmg, wmd)


def _gla_masks(reverse):
    c = GLA_CHUNK
    i = np.arange(c)[:, None]
    j = np.arange(c)[None, :]
    q_half, k_half = (0, 1) if reverse else (1, 0)
    masks = []
    for s in GLA_LEVELS:
        same_parent = (i // (2 * s)) == (j // (2 * s))
        masks.append(same_parent & ((i // s) % 2 == q_half) & ((j // s) % 2 == k_half))
    masks.append(i == j)
    return jnp.asarray(np.stack(masks).astype(np.float32))


def _gla_kernel(qf_ref, kf_ref, vf_ref, bf_ref, qb_ref, kb_ref, vb_ref, bb_ref, mask_ref,
                of_ref, ob_ref, state_ref, b32_ref, *, tile):
    c = GLA_CHUNK
    n_chunks = tile // c
    refs = {False: (qf_ref, kf_ref, vf_ref, bf_ref, of_ref), True: (qb_ref, kb_ref, vb_ref, bb_ref, ob_ref)}
    chains = [(reverse, h) for reverse in (False, True) for h in range(GLA_HEADS)]

    @pl.when(pl.program_id(1) == 0)
    def _():
        state_ref[...] = jnp.zeros_like(state_ref)

    sub8 = lax.broadcasted_iota(jnp.int32, (GLA_SUB, GLA_HK), 0)

    def chunk_body(ci, carry):
        row0 = {False: pl.multiple_of(ci * c, c), True: pl.multiple_of((n_chunks - 1 - ci) * c, c)}
        staged = []
        for reverse, h in chains:
            q_ref, k_ref, v_ref, b_ref, o_ref = refs[reverse]
            d = int(reverse)
            rows = pl.ds(row0[reverse], c)
            scan_last = 0 if reverse else c - 1
            lk = slice(h * GLA_HK, (h + 1) * GLA_HK)
            lv = slice(h * GLA_HV, (h + 1) * GLA_HV)
            q = q_ref[0, rows, lk].astype(F32)
            k = k_ref[0, rows, lk].astype(F32)
            v = v_ref[0, rows, lv]
            b = b_ref[0, rows, lk]
            b32_ref[d, h] = b

            def b_row(r, d=d, h=h):
                return b32_ref[d, h, pl.ds(r, 1), :]

            b_tot = b_row(scan_last)

            state = state_ref[d, h]
            q_in = (q * jnp.exp(b)).astype(BF16)
            o_inter = jnp.dot(q_in, state.astype(BF16), preferred_element_type=F32)

            diag = jnp.sum(q * k, axis=1, keepdims=True)
            level_scores = []
            for s in GLA_LEVELS:
                off = s if reverse else s - 1
                blocks = []
                for jb in range(c // GLA_SUB):
                    r0 = jb * GLA_SUB
                    if 2 * s >= GLA_SUB:
                        blk = jnp.broadcast_to(b_row(r0 // (2 * s) * (2 * s) + off), (GLA_SUB, GLA_HK))
                    else:
                        blk = jnp.broadcast_to(b_row(r0 + off), (GLA_SUB, GLA_HK))
                        for t in range(1, GLA_SUB // (2 * s)):
                            blk = jnp.where(sub8 >= 2 * s * t, b_row(r0 + 2 * s * t + off), blk)
                    blocks.append(blk)
                ref = jnp.concatenate(blocks, axis=0)
                e = jnp.exp(-jnp.abs(b - ref))
                level_scores.append(lax.dot_general(
                    (q * e).astype(BF16), (k * e).astype(BF16), NT_DIMS, preferred_element_type=F32))

            k_hat = (k * jnp.exp(b_tot - b)).astype(BF16)
            delta = lax.dot_general(k_hat, v, TN_DIMS, preferred_element_type=F32)
            decay_col = jnp.transpose(jnp.broadcast_to(jnp.exp(b_tot), (GLA_HK, GLA_HK)))
            decay = jnp.concatenate([decay_col] * (GLA_HV // GLA_HK), axis=1)
            staged.append((d, h, o_ref, rows, lv, v, o_inter, diag, level_scores, state * decay + delta))

        for d, h, o_ref, rows, lv, v, o_inter, diag, level_scores, new_state in staged:
            scores = mask_ref[d, len(GLA_LEVELS)] * diag
            for lvl, ls in enumerate(level_scores):
                scores = scores + mask_ref[d, lvl] * ls
            o = o_inter + jnp.dot(scores.astype(BF16), v, preferred_element_type=F32)
            o_ref[0, rows, lv] = o.astype(o_ref.dtype)
            state_ref[d, h] = new_state
        return carry

    lax.fori_loop(0, n_chunks, chunk_body, 0, unroll=2)


def _gla(q, k, v, bcum, *, tile):
    bsz, seq, _ = q.shape
    nt = seq // tile
    masks = jnp.stack([_gla_masks(False), _gla_masks(True)])
    fwd = lambda b, t: (b, t, 0)
    bwd = lambda b, t: (b, nt - 1 - t, 0)
    bwd_b = lambda b, t: (b, nt - 1 - t, 1)
    kernel = functools.partial(_gla_kernel, tile=tile)
    out = jax.ShapeDtypeStruct((bsz, seq, GLA_DV), BF16)
    return pl.pallas_call(
        kernel,
        grid=(bsz, nt),
        in_specs=[
            pl.BlockSpec((1, tile, GLA_DK), fwd), pl.BlockSpec((1, tile, GLA_DK), fwd),
            pl.BlockSpec((1, tile, GLA_DV), fwd), pl.BlockSpec((1, tile, GLA_DK), fwd),
            pl.BlockSpec((1, tile, GLA_DK), bwd), pl.BlockSpec((1, tile, GLA_DK), bwd),
            pl.BlockSpec((1, tile, GLA_DV), bwd), pl.BlockSpec((1, tile, GLA_DK), bwd_b),
            _const_spec(masks.shape),
        ],
        out_specs=[pl.BlockSpec((1, tile, GLA_DV), fwd), pl.BlockSpec((1, tile, GLA_DV), bwd)],
        out_shape=[out, out],
        scratch_shapes=[
            pltpu.VMEM((2, GLA_HEADS, GLA_HK, GLA_HV), F32),
            pltpu.VMEM((2, GLA_HEADS, GLA_CHUNK, GLA_HK), F32),
        ],
        compiler_params=pltpu.CompilerParams(
            dimension_semantics=("parallel", "arbitrary"), vmem_limit_bytes=VMEM_LIMIT_BYTES),
        name="gla",
    )(q, k, v, bcum, q, k, v, bcum, masks)


def _attn_kernel(q_ref, k_ref, vt_ref, lq_ref, lk_ref, gain_ref, o_ref,
                 q2_ref, s_ref, smax_ref, m_ref, acc_ref, *, tq, tk, seq, lam_init):
    n_tiles = seq // tk
    n_q = seq // tq
    kc = min(MXU_DEPTH, tk)
    n_kc = tk // kc
    ones_rows = jnp.ones((ONES_ROWS, kc), BF16)
    lane = lax.broadcasted_iota(jnp.int32, (tq, LANES), 1)

    def stage_queries(qi):
        q = q_ref[0, pl.ds(pl.multiple_of(qi * tq, tq), tq), :]
        zero = jnp.zeros_like(q)
        q2_ref[0:tq] = jnp.where(lane < DIFF_HEAD_DIM, q, zero)
        q2_ref[tq:2 * tq] = jnp.where(lane >= DIFF_HEAD_DIM, q, zero)

    def reset_stats():
        m_ref[...] = jnp.full_like(m_ref, -jnp.inf)
        acc_ref[...] = jnp.zeros_like(acc_ref)

    def scores_chunk(kt, slot, c, smax):
        ks = pl.multiple_of(kt * tk, tk)
        k = k_ref[0, pl.ds(ks + c * kc, kc), :]
        s = lax.dot_general(k, q2_ref[...], NT_DIMS, preferred_element_type=F32)
        s_ref[slot, c * kc:(c + 1) * kc, :] = s
        cmax = jnp.max(s, axis=0, keepdims=True)
        return cmax if smax is None else jnp.maximum(smax, cmax)

    def values_chunk(kt, slot, c, m_new, pv):
        ks = pl.multiple_of(kt * tk, tk)
        vt = jnp.concatenate([vt_ref[0, :, pl.ds(ks + c * kc, kc)], ones_rows], axis=0)
        p = jnp.exp2(s_ref[slot, c * kc:(c + 1) * kc, :] - m_new).astype(BF16)
        d = jnp.dot(vt, p, preferred_element_type=F32)
        return d if pv is None else pv + d

    def step(i, slot, next_kt):
        m_prev = m_ref[...]
        m_new = jnp.maximum(m_prev, smax_ref[slot])
        alpha = jnp.exp2(m_prev - m_new)
        m_ref[...] = m_new
        smax, pv = None, None
        for c in range(n_kc):
            smax = scores_chunk(next_kt, 1 - slot, c, smax)
            pv = values_chunk(i, slot, c, m_new, pv)
        smax_ref[1 - slot] = smax
        acc_ref[...] = alpha * acc_ref[...] + pv

    dots = jnp.sum(lq_ref[...] * lk_ref[...], axis=1, keepdims=True)
    e = jnp.exp(dots)
    lam = e[0:1, :] - e[1:2, :] + lam_init

    def finalize(qi):
        o2 = acc_ref[0:DIFF_HV] / acc_ref[DIFF_HV:DIFF_HV + 1]
        ot = o2[:, 0:tq] - lam * o2[:, tq:2 * tq]
        ms = jnp.mean(ot * ot, axis=0, keepdims=True)
        on = jnp.transpose(ot * lax.rsqrt(ms + EPS))
        o_ref[0, pl.ds(pl.multiple_of(qi * tq, tq), tq), :] = (
            on * gain_ref[...] * (1.0 - lam_init)).astype(o_ref.dtype)

    stage_queries(0)
    reset_stats()
    smax0 = None
    for c in range(n_kc):
        smax0 = scores_chunk(0, 0, c, smax0)
    smax_ref[0] = smax0

    def query_tile(qi, carry):
        def pair(j, c2):
            i = 2 * j
            step(i, 0, i + 1)
            step(i + 1, 1, i + 2)
            return c2

        lax.fori_loop(0, (n_tiles - 2) // 2, pair, 0)
        step(n_tiles - 2, 0, n_tiles - 1)
        stage_queries(jnp.minimum(qi + 1, n_q - 1))
        step(n_tiles - 1, 1, 0)
        finalize(qi)
        reset_stats()
        return carry

    lax.fori_loop(0, n_q, query_tile, 0)


def _attn(dq, dk, dvt, lq, lk, gain, *, lam_init, tq, tk):
    bsz, seq, _ = dq.shape
    assert seq % (2 * tk) == 0 and seq % tq == 0
    kernel = functools.partial(_attn_kernel, tq=tq, tk=tk, seq=seq, lam_init=lam_init)
    head_rows = pl.BlockSpec((1, seq, DIFF_HV), lambda b, h: (b, 0, h))
    return pl.pallas_call(
        kernel,
        grid=(bsz, DIFF_HEADS),
        in_specs=[
            head_rows, head_rows,
            pl.BlockSpec((1, DIFF_HV, seq), lambda b, h: (b, h, 0)),
            pl.BlockSpec((2, DIFF_HEAD_DIM), lambda b, h: (0, 0)),
            pl.BlockSpec((2, DIFF_HEAD_DIM), lambda b, h: (0, 0)),
            pl.BlockSpec((1, DIFF_HV), lambda b, h: (0, 0)),
        ],
        out_specs=head_rows,
        out_shape=jax.ShapeDtypeStruct((bsz, seq, D_MODEL), BF16),
        scratch_shapes=[
            pltpu.VMEM((2 * tq, DIFF_HV), BF16),
            pltpu.VMEM((2, tk, 2 * tq), F32),
            pltpu.VMEM((2, 1, 2 * tq), F32),
            pltpu.VMEM((1, 2 * tq), F32),
            pltpu.VMEM((DIFF_HV + ONES_ROWS, 2 * tq), F32),
        ],
        compiler_params=pltpu.CompilerParams(
            dimension_semantics=("parallel", "parallel"),
            vmem_limit_bytes=VMEM_LIMIT_BYTES),
        name="diff_attn",
    )(dq, dk, dvt, lq, lk, gain)


def _out_kernel(x_ref, gate_ref, of_ref, ob_ref, z_ref, od_ref, dz_ref, mg_ref, md_ref,
                ggain_ref, fgain_ref, wbg_ref, wbd_ref, wo_ref, y_ref):
    tile = x_ref.shape[1]
    parts = [slice(r, r + tile // 2) for r in (0, tile // 2)]
    ggain = ggain_ref[...]
    y_gla, y_diff = [], []
    for rows in parts:
        og = of_ref[0, rows].astype(F32) + ob_ref[0, rows].astype(F32)
        heads = []
        for h in range(GLA_HEADS):
            oh = og[:, h * GLA_HV:(h + 1) * GLA_HV]
            ms = jnp.mean(oh * oh, axis=1, keepdims=True)
            heads.append(oh * lax.rsqrt(ms + EPS) * ggain)
        og = jnp.concatenate(heads, axis=1) * z_ref[0, rows].astype(F32)
        y_gla.append(jnp.dot(og.astype(BF16), wbg_ref[...], preferred_element_type=F32))
        od = od_ref[0, rows].astype(F32) * dz_ref[0, rows].astype(F32)
        y_diff.append(jnp.dot(od.astype(BF16), wbd_ref[...], preferred_element_type=F32))
    ys = []
    for rows, yg, yd in zip(parts, y_gla, y_diff):
        merged = mg_ref[0, rows].astype(F32) * yg + md_ref[0, rows].astype(F32) * yd
        ys.append(jnp.dot(merged.astype(BF16), wo_ref[...], preferred_element_type=F32))
    for rows, y in zip(parts, ys):
        r = x_ref[0, rows] + gate_ref[0] * y
        ms = jnp.mean(r * r, axis=1, keepdims=True)
        y_ref[0, rows] = r * lax.rsqrt(ms + EPS) * fgain_ref[...]


def _out_stage(x, gate, o_f, o_b, z, o_d, dz, mg, md, ggain, fgain, wbg, wbd, wo, *, tile):
    bsz, seq, _ = x.shape
    row = pl.BlockSpec((1, tile, D_MODEL), lambda b, i: (b, i, 0))
    per_b = pl.BlockSpec((1, 1, D_MODEL), lambda b, i: (b, 0, 0))
    return pl.pallas_call(
        _out_kernel,
        grid=(bsz, seq // tile),
        in_specs=[row, per_b] + [row] * 7 + [
            _const_spec(ggain.shape), _const_spec(fgain.shape),
            _const_spec(wbg.shape), _const_spec(wbd.shape), _const_spec(wo.shape)],
        out_specs=row,
        out_shape=jax.ShapeDtypeStruct((bsz, seq, D_MODEL), F32),
        compiler_params=pltpu.CompilerParams(
            dimension_semantics=("parallel", "parallel"), vmem_limit_bytes=VMEM_LIMIT_BYTES),
        name="out_stage",
    )(x, gate, o_f, o_b, z, o_d, dz, mg, md, ggain, fgain, wbg, wbd, wo)


def _rope_tables(seq):
    half = DIFF_HEAD_DIM // 2
    inv = 1.0 / (ROPE_THETA ** (jnp.arange(half, dtype=F32) / half))
    ang = jnp.arange(seq, dtype=F32)[:, None] * inv[None, :]
    cos = jnp.cos(ang)
    sin = jnp.sin(ang)
    cs = jnp.concatenate([cos, cos, cos, cos], axis=1)
    sn = jnp.concatenate([-sin, sin, -sin, sin], axis=1)
    return cs, sn


def _layer_weights(w_in, w_alpha, b_alpha):
    splits = (GLA_DK, GLA_DK, GLA_DV, GLA_DV, 2 * GLA_RANK) + (D_MODEL,) * 6
    pts = np.cumsum(splits)[:-1].tolist()
    a_q, a_k, a_v, a_z, a_low, d_q, d_k, d_v, d_z, m_g, m_d = jnp.split(w_in, pts, axis=1)
    wg = jnp.concatenate([a_q, a_k, a_v, a_z], axis=1).astype(BF16)
    zeros = jnp.zeros((GLA_RANK, GLA_DK), w_alpha.dtype)
    walpha = jnp.concatenate([
        jnp.concatenate([w_alpha[0], zeros], axis=1),
        jnp.concatenate([zeros, w_alpha[1]], axis=1)], axis=0).astype(BF16)
    balpha = b_alpha.reshape(1, 2 * GLA_DK)
    return (wg, a_low.astype(BF16), walpha, balpha,
            d_q.astype(BF16), d_k.astype(BF16), d_v.T.astype(BF16), d_z.astype(BF16),
            m_g.astype(BF16), m_d.astype(BF16))


def _trunk_layer(x, mod, lam_init, norm_gain, in_weights, gla_norm_gain, lambda_q, lambda_k,
                 diff_norm_gain, wbg, wbd, wo, final_gain, *, row_tile, gla_tile, tq, tk):
    bsz, seq, _ = x.shape
    row_tile, gla_tile, tq, tk = (min(t, s) for t, s in
                                  ((row_tile, seq), (gla_tile, seq), (tq, seq), (tk, seq // 2)))
    shift, scale, gate = [m.reshape(bsz, 1, D_MODEL) for m in jnp.split(mod, 3, axis=-1)]
    cs, sn = _rope_tables(seq)
    (q, k, v, z, bcum, dq, dk, dv, dz, mg, md) = _inproj(
        x, shift, scale, norm_gain.reshape(1, D_MODEL), cs, sn, in_weights, row_tile)
    o_f, o_b = _gla(q, k, v, bcum, tile=gla_tile)
    o_d = _attn(dq, dk, dv, lambda_q, lambda_k, diff_norm_gain.reshape(1, DIFF_HV),
                lam_init=lam_init, tq=tq, tk=tk)
    return _out_stage(x, gate, o_f, o_b, z, o_d, dz, mg, md,
                      gla_norm_gain.reshape(1, GLA_HV), final_gain.reshape(1, D_MODEL),
                      wbg, wbd, wo, tile=row_tile)


def kernel(x_prompt, x_sample, c_prompt, c_sample, w_ada, b_ada, norm_gain, w_in, w_alpha, b_alpha,
           gla_norm_gain, lambda_q, lambda_k, diff_norm_gain, w_bo_gla, w_bo_diff, w_out, final_gain):
    depth = w_in.shape[0]
    assert depth == 1, "final RMSNorm is fused into the (single) layer's output stage"
    layer = 0
    lam_init = 0.8 - 0.6 * math.exp(-0.3 * layer)
    nb_p, nb_s = c_prompt.shape[0], c_sample.shape[0]
    rows = -(-(nb_p + nb_s) // 8) * 8
    c_all = jnp.concatenate(
        [c_prompt, c_sample, jnp.zeros((rows - nb_p - nb_s, D_MODEL), F32)], axis=0)
    mod = _ada(c_all, w_ada[layer].astype(BF16), b_ada[layer].reshape(1, 3 * D_MODEL))
    in_weights = _layer_weights(w_in[layer], w_alpha[layer], b_alpha[layer])
    common = dict(
        lam_init=lam_init, norm_gain=norm_gain[layer], in_weights=in_weights,
        gla_norm_gain=gla_norm_gain[layer], lambda_q=lambda_q[layer], lambda_k=lambda_k[layer],
        diff_norm_gain=diff_norm_gain[layer], wbg=w_bo_gla[layer].astype(BF16),
        wbd=w_bo_diff[layer].astype(BF16), wo=w_out[layer].astype(BF16), final_gain=final_gain)
    y_prompt = _trunk_layer(x_prompt, mod[:nb_p], row_tile=512, gla_tile=512, tq=256, tk=4096, **common)
    y_sample = _trunk_layer(x_sample, mod[nb_p:nb_p + nb_s], row_tile=512, gla_tile=512, tq=512, tk=2048,
                            **common)
    return (y_prompt, y_sample)
```
